```python
import math
import jax
import jax.numpy as jnp
from jax import lax
import numpy as np

D_MODEL = 1024
BATCH = 16
SEQ = 4096
DEPTH = 4

N_MIXERS = 2
N_HYENA = (DEPTH + 1) // 2
N_GLA = DEPTH // 2
N_MOD = 6
EPS = 1e-6

SHORT_CONV = 3
FILTER_BANDS = 16
FILTER_EMB = 1 + 2 * FILTER_BANDS
FILTER_HIDDEN = 64
DECAY_TARGET = 1e-2
FAST_DECAY_PCT = 0.3
SLOW_DECAY_PCT = 1.5
MIN_DECAY = math.log(DECAY_TARGET) / SLOW_DECAY_PCT
MAX_DECAY = math.log(DECAY_TARGET) / FAST_DECAY_PCT

GLA_HEADS = 4
GLA_DK = D_MODEL // 2
GLA_DV = D_MODEL
GLA_HEAD_K = GLA_DK // GLA_HEADS
GLA_HEAD_V = GLA_DV // GLA_HEADS
GLA_GATE_RANK = 16
GLA_TAU = 16.0
GLA_CHUNK = 64
GLA_IN = 2 * GLA_DK + 2 * GLA_DV + 2 * GLA_GATE_RANK

N_EXPERTS = 16
EC_CAPACITY = 2
D_EXPERT = 2 * D_MODEL

kernel_name = "hybrid_hyena_gla_ecmoe_encoder"


def rms_norm(x, g):
    xf = x.astype(jnp.float32)
    y = xf * lax.rsqrt(jnp.mean(xf * xf, axis=-1, keepdims=True) + EPS)
    return (y * g.astype(jnp.float32)).astype(x.dtype)


def modulate(h, shift, scale):
    return h * (1 + scale[:, None, :]) + shift[:, None, :]


def short_conv(u, w, b):
    C = u.shape[-1]
    y = lax.conv_general_dilated(
        u, w[:, None, :].astype(u.dtype), window_strides=(1,),
        padding=[(SHORT_CONV // 2, SHORT_CONV // 2)],
        dimension_numbers=('NWC', 'WIO', 'NWC'), feature_group_count=C)
    return y + b


def hyena_positions(L):
    t = jnp.linspace(0.0, 1.0, L, dtype=jnp.float32)
    bands = jnp.linspace(1e-4, FILTER_BANDS - 1, FILTER_BANDS, dtype=jnp.float32)
    w = (2.0 * math.pi / L) * jnp.arange(L, dtype=jnp.float32)[:, None] * bands[None, :]
    z = jnp.concatenate([t[:, None], jnp.cos(w), -jnp.sin(w)], axis=-1)
    return z, t


def hyena_filter_rfft(z, t, f_w1, f_b1, f_w2, f_b2, f_w3, f_freq):
    L = z.shape[0]
    D = f_w3.shape[-1] // 2
    freq = f_freq.astype(jnp.float32)
    h = jnp.sin(freq * (z @ f_w1.astype(jnp.float32) + f_b1.astype(jnp.float32)))
    h = jnp.sin(freq * (h @ f_w2.astype(jnp.float32) + f_b2.astype(jnp.float32)))
    h = (h @ f_w3.astype(jnp.float32)).reshape(L, 2, D)
    deltas = jnp.abs(jnp.linspace(MIN_DECAY, MAX_DECAY, D, dtype=jnp.float32))
    h = h * jnp.exp(-t[:, None, None] * deltas[None, None, :])
    fwd = h[:, 0]
    bwd = h[1:, 1][::-1]
    filt = jnp.concatenate([fwd, jnp.zeros((1, D), jnp.float32), bwd], axis=0)
    filt = filt / jnp.sum(jnp.abs(filt), axis=0, keepdims=True)
    return jnp.fft.rfft(filt, axis=0)


def hyena_mixer(h, z, t, w_in, conv_w, conv_b, f_w1, f_b1, f_w2, f_b2, f_w3, f_freq,
                skip, w_out):
    L = h.shape[1]
    proj = short_conv(h @ w_in, conv_w, conv_b)
    x0, x1, v = jnp.split(proj, 3, axis=-1)
    filt_f = hyena_filter_rfft(z, t, f_w1, f_b1, f_w2, f_b2, f_w3, f_freq)
    u = (v * x1).astype(jnp.float32)
    U = jnp.fft.rfft(u, n=2 * L, axis=1)
    y = jnp.fft.irfft(U * filt_f[None], n=2 * L, axis=1)[:, :L] + u * skip.astype(jnp.float32)
    return (y.astype(h.dtype) * x0) @ w_out


def gla_direction(q, k, v, g, strict):
    B, H, S, dk = q.shape
    dv = v.shape[-1]
    C = GLA_CHUNK
    n = S // C
    q, k, v, g = (a.reshape(B, H, n, C, a.shape[-1]) for a in (q, k, v, g))
    b = jnp.cumsum(g, axis=3)
    b_last = b[:, :, :, -1]
    q_dec = q * jnp.exp(b)
    k_inv = k * jnp.exp(-b)
    k_end = k * jnp.exp(b_last[:, :, :, None, :] - b)
    mask = jnp.tril(jnp.ones((C, C), dtype=bool), -1 if strict else 0)
    att = jnp.where(mask, jnp.einsum('bhnik,bhnjk->bhnij', q_dec, k_inv), 0.0)
    o_intra = jnp.einsum('bhnij,bhnjv->bhniv', att, v)

    def step(state, xs):
        qd, ke, vc, bl = xs
        o = jnp.einsum('bhik,bhkv->bhiv', qd, state)
        state = jnp.exp(bl)[..., None] * state + jnp.einsum('bhjk,bhjv->bhkv', ke, vc)
        return state, o

    xs = tuple(jnp.moveaxis(a, 2, 0) for a in (q_dec, k_end, v, b_last))
    _, o_inter = lax.scan(step, jnp.zeros((B, H, dk, dv), jnp.float32), xs)
    o = o_intra + jnp.moveaxis(o_inter, 0, 2)
    return o.reshape(B, H, S, dv)


def gla_mixer(h, w_in, w_gf, b_gf, w_gb, b_gb, norm_g, w_out):
    B, S, _ = h.shape
    cuts = [GLA_DK, 2 * GLA_DK, 2 * GLA_DK + GLA_DV, 2 * GLA_DK + 2 * GLA_DV,
            2 * GLA_DK + 2 * GLA_DV + GLA_GATE_RANK]
    q, k, v, r, lf, lb = jnp.split(h @ w_in, cuts, axis=-1)
    gf = jax.nn.log_sigmoid((lf @ w_gf + b_gf).astype(jnp.float32)) / GLA_TAU
    gb = jax.nn.log_sigmoid((lb @ w_gb + b_gb).astype(jnp.float32)) / GLA_TAU

    def heads(a, d):
        return a.astype(jnp.float32).reshape(B, S, GLA_HEADS, d).transpose(0, 2, 1, 3)

    q = heads(q, GLA_HEAD_K) * (GLA_HEAD_K ** -0.5)
    k = heads(k, GLA_HEAD_K)
    v = heads(v, GLA_HEAD_V)
    gf = heads(gf, GLA_HEAD_K)
    gb = heads(gb, GLA_HEAD_K)
    o_f = gla_direction(q, k, v, gf, strict=False)
    flip = lambda a: jnp.flip(a, axis=2)
    o_b = flip(gla_direction(flip(q), flip(k), flip(v), flip(gb), strict=True))
    o = (o_f + o_b).transpose(0, 2, 1, 3)
    o = o * lax.rsqrt(jnp.mean(o * o, axis=-1, keepdims=True) + EPS) * norm_g.astype(jnp.float32)
    o = o.reshape(B, S, GLA_DV).astype(h.dtype) * jax.nn.silu(r)
    return o @ w_out


def ec_moe(h, w_router, w_gate, w_up, w_down):
    B, S, D = h.shape
    cap = EC_CAPACITY * S // N_EXPERTS
    aff = jax.nn.softmax((h @ w_router).astype(jnp.float32), axis=-1)
    g, idx = lax.top_k(jnp.swapaxes(aff, 1, 2), cap)
    xg = jax.vmap(lambda hb, ib: hb[ib])(h, idx)
    a = jnp.einsum('becd,edf->becf', xg, w_gate)
    u = jnp.einsum('becd,edf->becf', xg, w_up)
    y = jnp.einsum('becf,efd->becd', jax.nn.silu(a) * u, w_down) * g[..., None].astype(h.dtype)
    return jax.vmap(
        lambda ib, yb: jnp.zeros((S, D), yb.dtype).at[ib.reshape(-1)].add(yb.reshape(-1, D))
    )(idx, y)


def setup_inputs(seed: int = 0) -> dict:
    key = jax.random.key(seed)
    ks = iter(jax.random.split(key, 40))
    D = D_MODEL

    def nrm(shape, scale):
        return jax.random.normal(next(ks), shape, jnp.float32) * scale

    return {
        "x": nrm((BATCH, SEQ, D), 1.0),
        "c": nrm((BATCH, D), 1.0),
        "w_ada": nrm((DEPTH, D, N_MOD * D), 0.5 * D ** -0.5),
        "b_ada": nrm((DEPTH, N_MOD * D), 0.02),
        "norm_mix": 1.0 + nrm((DEPTH, D), 0.02),
        "norm_ffn": 1.0 + nrm((DEPTH, D), 0.02),
        "hy_w_in": nrm((N_HYENA, D, 3 * D), D ** -0.5),
        "hy_conv_w": nrm((N_HYENA, SHORT_CONV, 3 * D), SHORT_CONV ** -0.5),
        "hy_conv_b": nrm((N_HYENA, 3 * D), 0.02),
        "hy_f_w1": nrm((N_HYENA, FILTER_EMB, FILTER_HIDDEN), FILTER_EMB ** -0.5),
        "hy_f_b1": nrm((N_HYENA, FILTER_HIDDEN), 0.02),
        "hy_f_w2": nrm((N_HYENA, FILTER_HIDDEN, FILTER_HIDDEN), FILTER_HIDDEN ** -0.5),
        "hy_f_b2": nrm((N_HYENA, FILTER_HIDDEN), 0.02),
        "hy_f_w3": nrm((N_HYENA, FILTER_HIDDEN, 2 * D), FILTER_HIDDEN ** -0.5),
        "hy_f_freq": 1.0 + nrm((N_HYENA, FILTER_HIDDEN), 0.02),
        "hy_skip": nrm((N_HYENA, D), 0.5),
        "hy_w_out": nrm((N_HYENA, D, D), D ** -0.5),
        "gla_w_in": nrm((N_GLA, D, GLA_IN), D ** -0.5),
        "gla_w_gf": nrm((N_GLA, GLA_GATE_RANK, GLA_DK), GLA_GATE_RANK ** -0.5),
        "gla_b_gf": nrm((N_GLA, GLA_DK), 0.02),
        "gla_w_gb": nrm((N_GLA, GLA_GATE_RANK, GLA_DK), GLA_GATE_RANK ** -0.5),
        "gla_b_gb": nrm((N_GLA, GLA_DK), 0.02),
        "gla_norm": 1.0 + nrm((N_GLA, GLA_HEADS, GLA_HEAD_V), 0.02),
        "gla_w_out": nrm((N_GLA, GLA_DV, D), GLA_DV ** -0.5),
        "moe_router": nrm((DEPTH, D, N_EXPERTS), D ** -0.5),
        "moe_w_gate": nrm((DEPTH, N_EXPERTS, D, D_EXPERT), D ** -0.5),
        "moe_w_up": nrm((DEPTH, N_EXPERTS, D, D_EXPERT), D ** -0.5),
        "moe_w_down": nrm((DEPTH, N_EXPERTS, D_EXPERT, D), D_EXPERT ** -0.5),
        "norm_final": 1.0 + nrm((D,), 0.02),
    }


def reference(x, c, w_ada, b_ada, norm_mix, norm_ffn,
              hy_w_in, hy_conv_w, hy_conv_b, hy_f_w1, hy_f_b1, hy_f_w2, hy_f_b2,
              hy_f_w3, hy_f_freq, hy_skip, hy_w_out,
              gla_w_in, gla_w_gf, gla_b_gf, gla_w_gb, gla_b_gb, gla_norm, gla_w_out,
              moe_router, moe_w_gate, moe_w_up, moe_w_down, norm_final):
    S = x.shape[1]
    z, t = hyena_positions(S)
    cond = jax.nn.silu(c)
    for i in range(DEPTH):
        j = i // N_MIXERS
        mod = cond @ w_ada[i] + b_ada[i]
        sh1, sc1, gt1, sh2, sc2, gt2 = jnp.split(mod, N_MOD, axis=-1)
        hn = modulate(rms_norm(x, norm_mix[i]), sh1, sc1)
        if i % N_MIXERS == 0:
            y = hyena_mixer(hn, z, t, hy_w_in[j], hy_conv_w[j], hy_conv_b[j],
                            hy_f_w1[j], hy_f_b1[j], hy_f_w2[j], hy_f_b2[j], hy_f_w3[j],
                            hy_f_freq[j], hy_skip[j], hy_w_out[j])
        else:
            y = gla_mixer(hn, gla_w_in[j], gla_w_gf[j], gla_b_gf[j], gla_w_gb[j],
                          gla_b_gb[j], gla_norm[j], gla_w_out[j])
        x = x + gt1[:, None, :] * y
        hn = modulate(rms_norm(x, norm_ffn[i]), sh2, sc2)
        x = x + gt2[:, None, :] * ec_moe(hn, moe_router[i], moe_w_gate[i], moe_w_up[i], moe_w_down[i])
    return rms_norm(x, norm_final)
```

```python
import functools
import math

import jax
import jax.numpy as jnp
from jax import lax
from jax.experimental import pallas as pl
from jax.experimental.pallas import tpu as pltpu

F32 = jnp.float32
BF16 = jnp.bfloat16
I32 = jnp.int32
HIGHEST = lax.Precision.HIGHEST

EPS = 1e-6
N_MOD = 6
LANES = 128
VMEM_LIMIT_CAP = 56 << 20

SHORT_CONV = 3
FILTER_BANDS = 16
DECAY_TARGET = 1e-2
FAST_DECAY_PCT = 0.3
SLOW_DECAY_PCT = 1.5
MIN_DECAY = math.log(DECAY_TARGET) / SLOW_DECAY_PCT
MAX_DECAY = math.log(DECAY_TARGET) / FAST_DECAY_PCT
CONV_BLOCK = 256


GLA_HEADS = 4
GLA_GATE_RANK = 16
GLA_TAU = 16.0
GLA_CHUNK = 64

N_EXPERTS = 16
EC_CAPACITY = 2


def _params(sem, vmem_mb):
    return pltpu.CompilerParams(
        dimension_semantics=sem,
        vmem_limit_bytes=min(vmem_mb << 20, VMEM_LIMIT_CAP))


def _sigmoid(x):
    return 1.0 / (1.0 + jnp.exp(-x))


def _silu(x):
    return x * _sigmoid(x)


def _log_sigmoid(x):
    return jnp.minimum(x, 0.0) - jnp.log(1.0 + jnp.exp(-jnp.abs(x)))


def _norm_mod(x, g, sc, sh):
    ms = jnp.mean(x * x, axis=-1, keepdims=True)
    return (x * lax.rsqrt(ms + EPS) * g) * (1.0 + sc) + sh


def _bdot(a, b):
    return jnp.dot(a, b, preferred_element_type=F32)


def _hdot(a, b):
    return jnp.dot(a, b, preferred_element_type=F32, precision=HIGHEST)


def _adaln_kernel(c_ref, w_ref, b_ref, o_ref):
    cond = _silu(c_ref[...])
    o_ref[0, 0] = _hdot(cond, w_ref[0]) + b_ref[0]


def _adaln(c, w_ada, b_ada):
    depth, d, _ = w_ada.shape
    b = c.shape[0]
    return pl.pallas_call(
        _adaln_kernel,
        out_shape=jax.ShapeDtypeStruct((depth, N_MOD, b, d), F32),
        grid=(depth, N_MOD),
        in_specs=[
            pl.BlockSpec((b, d), lambda i, n: (0, 0)),
            pl.BlockSpec((1, d, d), lambda i, n: (i, 0, n)),
            pl.BlockSpec((1, 1, d), lambda i, n: (i, 0, n)),
        ],
        out_specs=pl.BlockSpec((1, 1, b, d), lambda i, n: (i, n, 0, 0)),
        compiler_params=_params(("parallel", "parallel"), 32),
        name="adaln_mod",
    )(c, w_ada, b_ada.reshape(depth, 1, N_MOD * d))


def _hy_in_kernel(x_ref, xp_ref, xn_ref, g_ref, sh_ref, sc_ref, w_ref, cw_ref,
                  cb_ref, x0_ref, u_ref, *, tm, d):
    s = pl.program_id(1)
    ns = pl.num_programs(1)
    g = g_ref[...]
    sh = sh_ref[0]
    sc = sc_ref[0]
    hn = _norm_mod(x_ref[0], g, sc, sh).astype(BF16)
    hp = _norm_mod(xp_ref[0], g, sc, sh).astype(BF16)
    hx = _norm_mod(xn_ref[0], g, sc, sh).astype(BF16)
    row = lax.broadcasted_iota(I32, (tm, 1), 0)
    outs = []
    for part in range(3):
        cols = slice(part * d, (part + 1) * d)
        w = w_ref[:, cols]
        p = _bdot(hn, w)
        pp = jnp.where(s > 0, _bdot(hp, w)[7:8, :], 0.0)
        pn = jnp.where(s < ns - 1, _bdot(hx, w)[0:1, :], 0.0)
        p_prev = jnp.where(row == 0, pp, pltpu.roll(p, 1, 0))
        p_next = jnp.where(row == tm - 1, pn, pltpu.roll(p, tm - 1, 0))
        cw = cw_ref[:, cols]
        outs.append(cw[0:1] * p_prev + cw[1:2] * p + cw[2:3] * p_next
                    + cb_ref[:, cols])
    x0_ref[0] = outs[0]
    u_ref[0] = outs[2] * outs[1]


def _hy_in(x, g, sh, sc, w_bf, conv_w, conv_b, tm=512):
    b, s, d = x.shape
    tm = min(tm, s)
    hb = tm // 8
    nhb = s // 8
    kern = functools.partial(_hy_in_kernel, tm=tm, d=d)
    return pl.pallas_call(
        kern,
        out_shape=(jax.ShapeDtypeStruct((b, s, d), F32),
                   jax.ShapeDtypeStruct((b, s, d), F32)),
        grid=(b, s // tm),
        in_specs=[
            pl.BlockSpec((1, tm, d), lambda i, j: (i, j, 0)),
            pl.BlockSpec((1, 8, d), lambda i, j: (i, jnp.maximum(j * hb - 1, 0), 0)),
            pl.BlockSpec((1, 8, d), lambda i, j: (i, jnp.minimum((j + 1) * hb, nhb - 1), 0)),
            pl.BlockSpec((1, d), lambda i, j: (0, 0)),
            pl.BlockSpec((1, 1, d), lambda i, j: (i, 0, 0)),
            pl.BlockSpec((1, 1, d), lambda i, j: (i, 0, 0)),
            pl.BlockSpec((d, 3 * d), lambda i, j: (0, 0)),
            pl.BlockSpec((SHORT_CONV, 3 * d), lambda i, j: (0, 0)),
            pl.BlockSpec((1, 3 * d), lambda i, j: (0, 0)),
        ],
        out_specs=(pl.BlockSpec((1, tm, d), lambda i, j: (i, j, 0)),
                   pl.BlockSpec((1, tm, d), lambda i, j: (i, j, 0))),
        compiler_params=_params(("parallel", "parallel"), 56),
        name="hyena_in",
    )(x, x, x, g, sh, sc, w_bf, conv_w, conv_b)


FILT_T_COL = 127
FILT_VALID_COL = 126


def _hy_filter_kernel(z_ref, w1_ref, b1_ref, w2_ref, b2_ref, w3_ref, fr_ref,
                      dl_ref, gt_ref, nrm_ref):
    xb = pl.program_id(0)
    z = z_ref[...]
    fr = fr_ref[...]
    h = jnp.sin(fr * (_hdot(z, w1_ref[...]) + b1_ref[...]))
    h = jnp.sin(fr * (_hdot(h, w2_ref[...]) + b2_ref[...]))
    h = _hdot(h, w3_ref[0])
    t = z[:, FILT_T_COL:FILT_T_COL + 1]
    valid = z[:, FILT_VALID_COL:FILT_VALID_COL + 1]
    h = h * jnp.exp(-t * dl_ref[...]) * valid
    ht = h.T
    gt_ref[...] = ht
    part = jnp.sum(jnp.abs(ht), axis=1, keepdims=True)

    @pl.when(xb == 0)
    def _():
        nrm_ref[...] = jnp.zeros_like(nrm_ref)

    nrm_ref[...] += jnp.broadcast_to(part, nrm_ref.shape)


def _hy_filter(seq, w1, b1, w2, b2, w3, freq, xt=512):
    L = seq
    d = w3.shape[-1] // 2
    hid = w2.shape[0]
    emb = w1.shape[0]
    xt = min(xt, L)
    t = jnp.linspace(0.0, 1.0, L, dtype=F32)
    bands = jnp.linspace(1e-4, FILTER_BANDS - 1, FILTER_BANDS, dtype=F32)
    ang = (2.0 * math.pi / L) * jnp.arange(L, dtype=F32)[:, None] * bands[None, :]
    zf = jnp.concatenate([t[:, None], jnp.cos(ang), -jnp.sin(ang)], axis=-1)
    lag = jnp.abs(jnp.arange(2 * L) - L)
    valid = (lag < L).astype(F32)
    lagc = jnp.minimum(lag, L - 1)
    z2 = jnp.zeros((2 * L, LANES), F32)
    z2 = z2.at[:, :emb].set(zf[lagc])
    z2 = z2.at[:, FILT_T_COL].set(t[lagc])
    z2 = z2.at[:, FILT_VALID_COL].set(valid)
    w1p = jnp.zeros((LANES, hid), F32).at[:emb].set(w1)
    w3d = w3.reshape(hid, 2, d).transpose(1, 0, 2)
    deltas = jnp.abs(jnp.linspace(MIN_DECAY, MAX_DECAY, d, dtype=F32))[None, :]
    nxb = 2 * L // xt
    half = nxb // 2
    return pl.pallas_call(
        _hy_filter_kernel,
        out_shape=(jax.ShapeDtypeStruct((d, 2 * L), F32),
                   jax.ShapeDtypeStruct((d, LANES), F32)),
        grid=(nxb,),
        in_specs=[
            pl.BlockSpec((xt, LANES), lambda i: (i, 0)),
            pl.BlockSpec((LANES, hid), lambda i: (0, 0)),
            pl.BlockSpec((1, hid), lambda i: (0, 0)),
            pl.BlockSpec((hid, hid), lambda i: (0, 0)),
            pl.BlockSpec((1, hid), lambda i: (0, 0)),
            pl.BlockSpec((1, hid, d), lambda i: (jnp.where(i >= half, 0, 1), 0, 0)),
            pl.BlockSpec((1, hid), lambda i: (0, 0)),
            pl.BlockSpec((1, d), lambda i: (0, 0)),
        ],
        out_specs=(pl.BlockSpec((d, xt), lambda i: (0, i)),
                   pl.BlockSpec((d, LANES), lambda i: (0, 0))),
        compiler_params=_params(("arbitrary",), 32),
        name="hyena_filter",
    )(z2, w1p, b1[None, :], w2, b2[None, :], w3d, freq[None, :], deltas)


def _hy_conv_kernel(g_ref, nrm_ref, u_ref, y_ref, bank_ref, strip_ref, ucat_ref,
                    *, nb, bsz):
    P = CONV_BLOCK
    nshift = 2 * nb - 1
    nwin = 2 * nb * P // LANES
    ngrp = nshift * P // LANES
    zoff = nshift * P // LANES

    @pl.when(pl.program_id(0) == 0)
    def _():
        ucat_ref[...] = jnp.zeros_like(ucat_ref)

    row = lax.broadcasted_iota(I32, (LANES, LANES), 0)
    col = lax.broadcasted_iota(I32, (LANES, LANES), 1)
    upper = col >= row

    def fill_bank(w, cr):
        gw = g_ref[0, :, pl.ds(pl.multiple_of(w * LANES, LANES), LANES)]
        tile = jnp.broadcast_to(gw, (LANES, LANES))
        bank_ref[w] = pltpu.roll(tile, 0, 1, stride=1, stride_axis=0)
        return cr

    lax.fori_loop(0, nwin, fill_bank, 0)

    def fill_strip(m, cr):
        r0 = pl.multiple_of(m * LANES, LANES)
        for q in range(P // LANES):
            w = zoff + q - m
            blk = jnp.where(upper, bank_ref[w], bank_ref[w - 1])
            strip_ref[pl.ds(r0, LANES), q * LANES:(q + 1) * LANES] = blk.astype(BF16)
        return cr

    lax.fori_loop(0, ngrp, fill_strip, 0)

    u = u_ref[0].astype(BF16)
    rows = nb * bsz
    for m in range(nshift):
        dshift = (nb - 1 - m) * bsz
        if dshift >= 0:
            ucat_ref[dshift:rows, m * P:(m + 1) * P] = u[0:rows - dshift]
        else:
            ucat_ref[0:rows + dshift, m * P:(m + 1) * P] = u[-dshift:rows]
    y = _bdot(ucat_ref[...], strip_ref[...])
    y_ref[0] = y * (1.0 / nrm_ref[0, :, 0:1])


def _hy_conv(gt, nrm, ut, nb, bsz):
    d = ut.shape[0]
    P = CONV_BLOCK
    rows = nb * bsz
    nshift = 2 * nb - 1
    kern = functools.partial(_hy_conv_kernel, nb=nb, bsz=bsz)
    return pl.pallas_call(
        kern,
        out_shape=jax.ShapeDtypeStruct((d, rows, P), F32),
        grid=(d,),
        in_specs=[
            pl.BlockSpec((1, 1, 2 * nb * P), lambda i: (i, 0, 0)),
            pl.BlockSpec((1, 1, LANES), lambda i: (i, 0, 0)),
            pl.BlockSpec((1, rows, P), lambda i: (i, 0, 0)),
        ],
        out_specs=pl.BlockSpec((1, rows, P), lambda i: (i, 0, 0)),
        scratch_shapes=[
            pltpu.VMEM((2 * nb * P // LANES, LANES, LANES), F32),
            pltpu.VMEM((nshift * P, P), BF16),
            pltpu.VMEM((rows, nshift * P), BF16),
        ],
        compiler_params=_params(("arbitrary",), 48),
        name="hyena_conv",
    )(gt.reshape(d, 1, 2 * nb * P), nrm.reshape(d, 1, LANES), ut)


def _mixer_tail(a_bf, w_ref, x_ref, gt_ref, g2_ref, sh2_ref, sc2_ref, wr_ref,
                xo_ref, hn_ref, lg_ref):
    xn = x_ref[0] + gt_ref[0] * _bdot(a_bf, w_ref[...])
    xo_ref[0] = xn
    hn = _norm_mod(xn, g2_ref[...], sc2_ref[0], sh2_ref[0])
    hn_ref[0] = hn.astype(BF16)
    lg_ref[0] = _hdot(hn, wr_ref[...])


def _hy_out_kernel(y_ref, u_ref, x0_ref, skip_ref, w_ref, x_ref, gt_ref, g2_ref,
                   sh2_ref, sc2_ref, wr_ref, xo_ref, hn_ref, lg_ref):
    a = (y_ref[0] + u_ref[0] * skip_ref[...]) * x0_ref[0]
    _mixer_tail(a.astype(BF16), w_ref, x_ref, gt_ref, g2_ref, sh2_ref, sc2_ref,
                wr_ref, xo_ref, hn_ref, lg_ref)


def _tail_specs(b, s, d, tm):
    tok = pl.BlockSpec((1, tm, d), lambda i, j: (i, j, 0))
    vec = pl.BlockSpec((1, d), lambda i, j: (0, 0))
    bvec = pl.BlockSpec((1, 1, d), lambda i, j: (i, 0, 0))
    wsq = pl.BlockSpec((d, d), lambda i, j: (0, 0))
    wr = pl.BlockSpec((d, LANES), lambda i, j: (0, 0))
    out_shape = (jax.ShapeDtypeStruct((b, s, d), F32),
                 jax.ShapeDtypeStruct((b, s, d), BF16),
                 jax.ShapeDtypeStruct((b, s, LANES), F32))
    out_specs = (tok, tok, pl.BlockSpec((1, tm, LANES), lambda i, j: (i, j, 0)))
    return tok, vec, bvec, wsq, wr, out_shape, out_specs


def _hy_out(y, u, x0, skip, w_bf, x, gt1, g2, sh2, sc2, wr, tm=512):
    b, s, d = x.shape
    tm = min(tm, s)
    tok, vec, bvec, wsq, wrs, out_shape, out_specs = _tail_specs(b, s, d, tm)
    return pl.pallas_call(
        _hy_out_kernel,
        out_shape=out_shape,
        grid=(b, s // tm),
        in_specs=[tok, tok, tok, vec, wsq, tok, bvec, vec, bvec, bvec, wrs],
        out_specs=out_specs,
        compiler_params=_params(("parallel", "parallel"), 48),
        name="hyena_out",
    )(y, u, x0, skip, w_bf, x, gt1, g2, sh2, sc2, wr)


def _gla_in_kernel(x_ref, g_ref, sh_ref, sc_ref, w_ref, wl_ref, wg_ref, bg_ref,
                   q_ref, k_ref, v_ref, r_ref, gf_ref, gb_ref, *, dk, dv, qscale):
    hn = _norm_mod(x_ref[0], g_ref[...], sc_ref[0], sh_ref[0]).astype(BF16)
    q_ref[0] = _bdot(hn, w_ref[:, 0:dk]) * qscale
    k_ref[0] = _bdot(hn, w_ref[:, dk:2 * dk])
    v_ref[0] = _bdot(hn, w_ref[:, 2 * dk:2 * dk + dv])
    r_ref[0] = _bdot(hn, w_ref[:, 2 * dk + dv:2 * dk + 2 * dv])
    low = _bdot(hn, wl_ref[...])
    gates = _log_sigmoid(_hdot(low, wg_ref[...]) + bg_ref[...]) * (1.0 / GLA_TAU)
    gf_ref[0] = gates[:, 0:dk]
    gb_ref[0] = gates[:, dk:2 * dk]


def _gla_in(x, g, sh, sc, w_main_bf, w_low_bf, w_gate, b_gate, dk, dv, tm=512):
    b, s, d = x.shape
    tm = min(tm, s)
    qscale = float((dk // GLA_HEADS) ** -0.5)
    kern = functools.partial(_gla_in_kernel, dk=dk, dv=dv, qscale=qscale)
    tokk = pl.BlockSpec((1, tm, dk), lambda i, j: (i, j, 0))
    tokv = pl.BlockSpec((1, tm, dv), lambda i, j: (i, j, 0))
    return pl.pallas_call(
        kern,
        out_shape=(jax.ShapeDtypeStruct((b, s, dk), F32),
                   jax.ShapeDtypeStruct((b, s, dk), F32),
                   jax.ShapeDtypeStruct((b, s, dv), F32),
                   jax.ShapeDtypeStruct((b, s, dv), F32),
                   jax.ShapeDtypeStruct((b, s, dk), F32),
                   jax.ShapeDtypeStruct((b, s, dk), F32)),
        grid=(b, s // tm),
        in_specs=[
            pl.BlockSpec((1, tm, d), lambda i, j: (i, j, 0)),
            pl.BlockSpec((1, d), lambda i, j: (0, 0)),
            pl.BlockSpec((1, 1, d), lambda i, j: (i, 0, 0)),
            pl.BlockSpec((1, 1, d), lambda i, j: (i, 0, 0)),
            pl.BlockSpec((d, 2 * dk + 2 * dv), lambda i, j: (0, 0)),
            pl.BlockSpec((d, LANES), lambda i, j: (0, 0)),
            pl.BlockSpec((LANES, 2 * dk), lambda i, j: (0, 0)),
            pl.BlockSpec((1, 2 * dk), lambda i, j: (0, 0)),
        ],
        out_specs=(tokk, tokk, tokv, tokv, tokk, tokk),
        compiler_params=_params(("parallel", "parallel"), 56),
        name="gla_in",
    )(x, g, sh, sc, w_main_bf, w_low_bf, w_gate, b_gate)


def _gla_core_kernel(q_ref, k_ref, v_ref, gf_ref, gb_ref, o_ref, sf_ref, sb_ref,
                     *, nchunk):
    C = GLA_CHUNK
    ri = lax.broadcasted_iota(I32, (C, C), 0)
    ci = lax.broadcasted_iota(I32, (C, C), 1)
    tri_lo = (ci <= ri).astype(F32)
    tri_up = (ci >= ri).astype(F32)
    mask_f = ci <= ri
    mask_b = ci > ri
    sf_ref[...] = jnp.zeros_like(sf_ref)
    sb_ref[...] = jnp.zeros_like(sb_ref)
    nt = (((1,), (1,)), ((), ()))
    tn = (((0,), (0,)), ((), ()))

    def one_chunk(n, tri, mask, g_ref, s_ref, edge, accumulate):
        rows = pl.ds(pl.multiple_of(n * C, C), C)
        q = q_ref[0, rows, :]
        k = k_ref[0, rows, :]
        v = v_ref[0, rows, :].astype(BF16)
        cum = _hdot(tri, g_ref[0, rows, :])
        tot = cum[edge:edge + 1, :]
        q_dec = (q * jnp.exp(cum)).astype(BF16)
        k_inv = (k * jnp.exp(-cum)).astype(BF16)
        k_end = (k * jnp.exp(tot - cum)).astype(BF16)
        att = lax.dot_general(q_dec, k_inv, nt, preferred_element_type=F32)
        att = jnp.where(mask, att, 0.0).astype(BF16)
        st = s_ref[...]
        o = _bdot(att, v) + lax.dot_general(q_dec, st.astype(BF16), nt,
                                            preferred_element_type=F32)
        s_ref[...] = st * jnp.exp(tot) + lax.dot_general(
            v, k_end, tn, preferred_element_type=F32)
        if accumulate:
            o_ref[0, rows, :] += o
        else:
            o_ref[0, rows, :] = o

    def fwd(n, carry):
        one_chunk(n, tri_lo, mask_f, gf_ref, sf_ref, C - 1, False)
        return carry

    def bwd(i, carry):
        one_chunk(nchunk - 1 - i, tri_up, mask_b, gb_ref, sb_ref, 0, True)
        return carry

    lax.fori_loop(0, nchunk, fwd, 0)
    lax.fori_loop(0, nchunk, bwd, 0)


def _gla_core(q, k, v, gf, gb):
    b, s, dk = q.shape
    dv = v.shape[-1]
    hk = dk // GLA_HEADS
    hv = dv // GLA_HEADS
    kern = functools.partial(_gla_core_kernel, nchunk=s // GLA_CHUNK)
    speck = pl.BlockSpec((1, s, hk), lambda i, h: (i, 0, h))
    specv = pl.BlockSpec((1, s, hv), lambda i, h: (i, 0, h))
    return pl.pallas_call(
        kern,
        out_shape=jax.ShapeDtypeStruct((b, s, dv), F32),
        grid=(b, GLA_HEADS),
        in_specs=[speck, speck, specv, speck, speck],
        out_specs=specv,
        scratch_shapes=[pltpu.VMEM((hv, hk), F32), pltpu.VMEM((hv, hk), F32)],
        compiler_params=_params(("parallel", "parallel"), 48),
        name="gla_core",
    )(q, k, v, gf, gb)


def _gla_out_kernel(o_ref, r_ref, ng_ref, w_ref, x_ref, gt_ref, g2_ref, sh2_ref,
                    sc2_ref, wr_ref, xo_ref, hn_ref, lg_ref, *, hv):
    o = o_ref[0]
    parts = []
    for h in range(GLA_HEADS):
        oh = o[:, h * hv:(h + 1) * hv]
        ms = jnp.mean(oh * oh, axis=-1, keepdims=True)
        parts.append(oh * lax.rsqrt(ms + EPS))
    on = jnp.concatenate(parts, axis=-1) * ng_ref[...]
    a = on * _silu(r_ref[0])
    _mixer_tail(a.astype(BF16), w_ref, x_ref, gt_ref, g2_ref, sh2_ref, sc2_ref,
                wr_ref, xo_ref, hn_ref, lg_ref)


def _gla_out(o, r, ng, w_bf, x, gt1, g2, sh2, sc2, wr, tm=512):
    b, s, d = x.shape
    dv = o.shape[-1]
    tm = min(tm, s)
    tok, vec, bvec, wsq, wrs, out_shape, out_specs = _tail_specs(b, s, d, tm)
    tokv = pl.BlockSpec((1, tm, dv), lambda i, j: (i, j, 0))
    kern = functools.partial(_gla_out_kernel, hv=dv // GLA_HEADS)
    return pl.pallas_call(
        kern,
        out_shape=out_shape,
        grid=(b, s // tm),
        in_specs=[tokv, tokv, pl.BlockSpec((1, dv), lambda i, j: (0, 0)),
                  pl.BlockSpec((dv, d), lambda i, j: (0, 0)),
                  tok, bvec, vec, bvec, bvec, wrs],
        out_specs=out_specs,
        compiler_params=_params(("parallel", "parallel"), 48),
        name="gla_out",
    )(o, r, ng, w_bf, x, gt1, g2, sh2, sc2, wr)


def _prefix_sum_lanes(x, n):
    lane = lax.broadcasted_iota(I32, x.shape, 1)
    sh = 1
    while sh < n:
        x = x + jnp.where(lane >= sh, pltpu.roll(x, sh, 1), 0)
        sh *= 2
    return x


def _route_kernel(lg_ref, slot_t_ref, slot_ref, aff_ref, *, cap, seq):
    E = N_EXPERTS
    lg = lg_ref[0]
    lane = lax.broadcasted_iota(I32, lg.shape, 1)
    live = lane < E
    lgm = jnp.where(live, lg, -jnp.inf)
    mx = jnp.max(lgm, axis=-1, keepdims=True)
    ex = jnp.where(live, jnp.exp(lgm - mx), 0.0)
    aff = ex / jnp.sum(ex, axis=-1, keepdims=True)
    aff_ref[0] = aff
    aff_t = aff.T[0:E, :]
    bits = pltpu.bitcast(aff_t, I32)

    def search(_, lohi):
        lo, hi = lohi
        mid = lo + (hi - lo + 1) // 2
        cnt = jnp.sum((bits >= mid).astype(I32), axis=1, keepdims=True)
        ok = cnt >= cap
        return jnp.where(ok, mid, lo), jnp.where(ok, hi, mid - 1)

    lo0 = jnp.zeros((E, 1), I32)
    hi0 = jnp.full((E, 1), 0x7F800000, I32)
    thr, _ = lax.fori_loop(0, 32, search, (lo0, hi0))
    gt = bits > thr
    eq = bits == thr
    need = cap - jnp.sum(gt.astype(I32), axis=1, keepdims=True)
    eq_i = eq.astype(I32)
    eq_rank = _prefix_sum_lanes(eq_i, seq) - eq_i
    sel = jnp.logical_or(gt, jnp.logical_and(eq, eq_rank < need))
    sel_i = sel.astype(I32)
    pos = _prefix_sum_lanes(sel_i, seq) - sel_i
    slot_t = jnp.where(sel, pos, -1)
    slot_t_ref[0] = slot_t
    pad = jnp.full((LANES - E, seq), -1, I32)
    slot_ref[0] = jnp.concatenate([slot_t, pad], axis=0).T


def _route(logits, cap):
    b, s, _ = logits.shape
    kern = functools.partial(_route_kernel, cap=cap, seq=s)
    return pl.pallas_call(
        kern,
        out_shape=(jax.ShapeDtypeStruct((b, N_EXPERTS, s), I32),
                   jax.ShapeDtypeStruct((b, s, LANES), I32),
                   jax.ShapeDtypeStruct((b, s, LANES), F32)),
        grid=(b,),
        in_specs=[pl.BlockSpec((1, s, LANES), lambda i: (i, 0, 0))],
        out_specs=(pl.BlockSpec((1, N_EXPERTS, s), lambda i: (i, 0, 0)),
                   pl.BlockSpec((1, s, LANES), lambda i: (i, 0, 0)),
                   pl.BlockSpec((1, s, LANES), lambda i: (i, 0, 0))),
        compiler_params=_params(("parallel",), 48),
        name="moe_route",
    )(logits)


def _moe_ffn_kernel(hn_ref, slot_ref, wg_ref, wu_ref, wd_ref, y_ref, xg_ref,
                    acc_ref, *, cap):
    f = pl.program_id(2)
    nf = pl.num_programs(2)

    @pl.when(f == 0)
    def _():
        slot = slot_ref[0, 0]
        r = lax.broadcasted_iota(I32, (cap, 1), 0)
        onehot = jnp.where(slot == r, 1.0, 0.0).astype(BF16)
        xg_ref[...] = _bdot(onehot, hn_ref[0]).astype(BF16)
        acc_ref[...] = jnp.zeros_like(acc_ref)

    xg = xg_ref[...]
    a = _bdot(xg, wg_ref[0])
    u = _bdot(xg, wu_ref[0])
    hh = (_silu(a) * u).astype(BF16)
    acc_ref[...] += _bdot(hh, wd_ref[0])

    @pl.when(f == nf - 1)
    def _():
        y_ref[0, 0] = acc_ref[...].astype(BF16)


def _moe_ffn(hn_bf, slot_t, wg_bf, wu_bf, wd_bf, cap, fc=512):
    b, s, d = hn_bf.shape
    e, _, fdim = wg_bf.shape
    fc = min(fc, fdim)
    kern = functools.partial(_moe_ffn_kernel, cap=cap)
    return pl.pallas_call(
        kern,
        out_shape=jax.ShapeDtypeStruct((b, e, cap, d), BF16),
        grid=(b, e, fdim // fc),
        in_specs=[
            pl.BlockSpec((1, s, d), lambda i, j, f: (i, 0, 0)),
            pl.BlockSpec((1, 1, 1, s), lambda i, j, f: (i, j, 0, 0)),
            pl.BlockSpec((1, d, fc), lambda i, j, f: (j, 0, f)),
            pl.BlockSpec((1, d, fc), lambda i, j, f: (j, 0, f)),
            pl.BlockSpec((1, fc, d), lambda i, j, f: (j, f, 0)),
        ],
        out_specs=pl.BlockSpec((1, 1, cap, d), lambda i, j, f: (i, j, 0, 0)),
        scratch_shapes=[pltpu.VMEM((cap, d), BF16), pltpu.VMEM((cap, d), F32)],
        compiler_params=_params(("parallel", "arbitrary", "arbitrary"), 56),
        name="moe_ffn",
    )(hn_bf, slot_t.reshape(b, e, 1, s), wg_bf, wu_bf, wd_bf)


def _moe_combine_kernel(x_ref, gt_ref, slot_ref, aff_ref, y_ref, gfin_ref, o_ref,
                        acc_ref, *, cap, final):
    e = pl.program_id(2)
    ne = pl.num_programs(2)

    @pl.when(e == 0)
    def _():
        acc_ref[...] = jnp.zeros_like(acc_ref)

    slot = slot_ref[0]
    lane = lax.broadcasted_iota(I32, slot.shape, 1)
    pick = lane == e
    slot_e = jnp.sum(jnp.where(pick, slot, 0), axis=1, keepdims=True)
    aff_e = jnp.sum(jnp.where(pick, aff_ref[0], 0.0), axis=1, keepdims=True)
    r = lax.broadcasted_iota(I32, (1, cap), 1)
    onehot = jnp.where(slot_e == r, 1.0, 0.0).astype(BF16)
    acc_ref[...] += aff_e * _bdot(onehot, y_ref[0, 0])

    @pl.when(e == ne - 1)
    def _():
        xn = x_ref[0] + gt_ref[0] * acc_ref[...]
        if final:
            ms = jnp.mean(xn * xn, axis=-1, keepdims=True)
            xn = xn * lax.rsqrt(ms + EPS) * gfin_ref[...]
        o_ref[0] = xn


def _moe_combine(x, gt2, slot, aff, y, gfin, cap, final, tt=1024):
    b, s, d = x.shape
    e = y.shape[1]
    tt = min(tt, s)
    kern = functools.partial(_moe_combine_kernel, cap=cap, final=final)
    return pl.pallas_call(
        kern,
        out_shape=jax.ShapeDtypeStruct((b, s, d), F32),
        grid=(b, s // tt, e),
        in_specs=[
            pl.BlockSpec((1, tt, d), lambda i, t, j: (i, t, 0)),
            pl.BlockSpec((1, 1, d), lambda i, t, j: (i, 0, 0)),
            pl.BlockSpec((1, tt, LANES), lambda i, t, j: (i, t, 0)),
            pl.BlockSpec((1, tt, LANES), lambda i, t, j: (i, t, 0)),
            pl.BlockSpec((1, 1, cap, d), lambda i, t, j: (i, j, 0, 0)),
            pl.BlockSpec((1, d), lambda i, t, j: (0, 0)),
        ],
        out_specs=pl.BlockSpec((1, tt, d), lambda i, t, j: (i, t, 0)),
        scratch_shapes=[pltpu.VMEM((tt, d), F32)],
        compiler_params=_params(("parallel", "parallel", "arbitrary"), 48),
        name="moe_combine",
    )(x, gt2, slot, aff, y, gfin)


def kernel(x, c, w_ada, b_ada, norm_mix, norm_ffn, hy_w_in, hy_conv_w, hy_conv_b, hy_f_w1, hy_f_b1, hy_f_w2, hy_f_b2, hy_f_w3, hy_f_freq, hy_skip, hy_w_out, gla_w_in, gla_w_gf, gla_b_gf, gla_w_gb, gla_b_gb, gla_norm, gla_w_out, moe_router, moe_w_gate, moe_w_up, moe_w_down, norm_final):
    b, s, d = x.shape
    depth = w_ada.shape[0]
    n_exp = moe_router.shape[-1]
    cap = EC_CAPACITY * s // n_exp
    dk = gla_w_gf.shape[-1]
    dv = gla_w_out.shape[1]
    nb = s // CONV_BLOCK

    mod = _adaln(c, w_ada, b_ada)
    for i in range(depth):
        j = i // 2
        sh1, sc1, gt1, sh2, sc2, gt2 = (mod[i, m].reshape(b, 1, d) for m in range(N_MOD))
        g1 = norm_mix[i][None, :]
        g2 = norm_ffn[i][None, :]
        wr = jnp.zeros((d, LANES), F32).at[:, :n_exp].set(moe_router[i])
        if i % 2 == 0:
            x0, u = _hy_in(x, g1, sh1, sc1, hy_w_in[j].astype(BF16), hy_conv_w[j],
                           hy_conv_b[j][None, :])
            gt, nrm = _hy_filter(s, hy_f_w1[j], hy_f_b1[j], hy_f_w2[j], hy_f_b2[j],
                                 hy_f_w3[j], hy_f_freq[j])
            ut = u.reshape(b, nb, CONV_BLOCK, d).transpose(3, 1, 0, 2)
            yt = _hy_conv(gt, nrm, ut.reshape(d, nb * b, CONV_BLOCK), nb, b)
            y = yt.reshape(d, nb, b, CONV_BLOCK).transpose(2, 1, 3, 0).reshape(b, s, d)
            x, hn, lg = _hy_out(y, u, x0, hy_skip[j][None, :], hy_w_out[j].astype(BF16),
                                x, gt1, g2, sh2, sc2, wr)
        else:
            w_in = gla_w_in[j]
            w_low = jnp.zeros((d, LANES), F32).at[:, :2 * GLA_GATE_RANK].set(
                w_in[:, 2 * dk + 2 * dv:]).astype(BF16)
            w_gate = jnp.zeros((LANES, 2 * dk), F32)
            w_gate = w_gate.at[:GLA_GATE_RANK, :dk].set(gla_w_gf[j])
            w_gate = w_gate.at[GLA_GATE_RANK:2 * GLA_GATE_RANK, dk:].set(gla_w_gb[j])
            b_gate = jnp.concatenate([gla_b_gf[j], gla_b_gb[j]])[None, :]
            q, k, v, r, gf, gb = _gla_in(x, g1, sh1, sc1,
                                         w_in[:, :2 * dk + 2 * dv].astype(BF16),
                                         w_low, w_gate, b_gate, dk, dv)
            o = _gla_core(q, k, v, gf, gb)
            x, hn, lg = _gla_out(o, r, gla_norm[j].reshape(1, dv),
                                 gla_w_out[j].astype(BF16), x, gt1, g2, sh2, sc2, wr)
        slot_t, slot, aff = _route(lg, cap)
        y = _moe_ffn(hn, slot_t, moe_w_gate[i].astype(BF16), moe_w_up[i].astype(BF16),
                     moe_w_down[i].astype(BF16), cap)
        x = _moe_combine(x, gt2, slot, aff, y, norm_final[None, :], cap,
                         final=(i == depth - 1))
    return x
```

```python
import functools
import math

import jax
import jax.numpy as jnp
from jax import lax
from jax.experimental import pallas as pl
from jax.experimental.pallas import tpu as pltpu

F32 = jnp.float32
BF16 = jnp.bfloat16
I32 = jnp.int32
HIGHEST = lax.Precision.HIGHEST

EPS = 1e-6
N_MOD = 6
LANES = 128
VMEM_LIMIT_CAP = 56 << 20

SHORT_CONV = 3
FILTER_BANDS = 16
DECAY_TARGET = 1e-2
FAST_DECAY_PCT = 0.3
SLOW_DECAY_PCT = 1.5
MIN_DECAY = math.log(DECAY_TARGET) / SLOW_DECAY_PCT
MAX_DECAY = math.log(DECAY_TARGET) / FAST_DECAY_PCT
CONV_BLOCK = 256


GLA_HEADS = 4
GLA_GATE_RANK = 16
GLA_TAU = 16.0
GLA_CHUNK = 64
GLA_GROUP = 4

N_EXPERTS = 16
EC_CAPACITY = 2
MOE_TILE = 256
MOE_SLOT_BLOCK = 128
MOE_WIN_TILES = 6
BF16_ROWS = 16


def _params(sem, vmem_mb):
    return pltpu.CompilerParams(
        dimension_semantics=sem,
        vmem_limit_bytes=min(vmem_mb << 20, VMEM_LIMIT_CAP))


def _sigmoid(x):
    return 1.0 / (1.0 + jnp.exp(-x))


def _silu(x):
    return x * _sigmoid(x)


def _log_sigmoid(x):
    return jnp.minimum(x, 0.0) - jnp.log(1.0 + jnp.exp(-jnp.abs(x)))


def _norm_mod(x, g, sc, sh):
    ms = jnp.mean(x * x, axis=-1, keepdims=True)
    return (x * lax.rsqrt(ms + EPS) * g) * (1.0 + sc) + sh


def _bdot(a, b):
    return jnp.dot(a, b, preferred_element_type=F32)


def _hdot(a, b):
    return jnp.dot(a, b, preferred_element_type=F32, precision=HIGHEST)


def _adaln_kernel(c_ref, w_ref, b_ref, o_ref):
    cond = _silu(c_ref[...])
    o_ref[0, 0] = _hdot(cond, w_ref[0]) + b_ref[0]


def _adaln(c, w_ada, b_ada):
    depth, d, _ = w_ada.shape
    b = c.shape[0]
    return pl.pallas_call(
        _adaln_kernel,
        out_shape=jax.ShapeDtypeStruct((depth, N_MOD, b, d), F32),
        grid=(depth, N_MOD),
        in_specs=[
            pl.BlockSpec((b, d), lambda i, n: (0, 0)),
            pl.BlockSpec((1, d, d), lambda i, n: (i, 0, n)),
            pl.BlockSpec((1, 1, d), lambda i, n: (i, 0, n)),
        ],
        out_specs=pl.BlockSpec((1, 1, b, d), lambda i, n: (i, n, 0, 0)),
        compiler_params=_params(("parallel", "parallel"), 32),
        name="adaln_mod",
    )(c, w_ada, b_ada.reshape(depth, 1, N_MOD * d))


def _hy_in_kernel(x_ref, xp_ref, xn_ref, g_ref, sh_ref, sc_ref, w_ref, cw_ref,
                  cb_ref, x0_ref, u_ref, *, tm, d):
    s = pl.program_id(1)
    ns = pl.num_programs(1)
    g = g_ref[...]
    sh = sh_ref[0]
    sc = sc_ref[0]
    hn = _norm_mod(x_ref[0], g, sc, sh).astype(BF16)
    hp = _norm_mod(xp_ref[0], g, sc, sh).astype(BF16)
    hx = _norm_mod(xn_ref[0], g, sc, sh).astype(BF16)
    row = lax.broadcasted_iota(I32, (tm, 1), 0)
    outs = []
    for part in range(3):
        cols = slice(part * d, (part + 1) * d)
        w = w_ref[:, cols]
        p = _bdot(hn, w)
        pp = jnp.where(s > 0, _bdot(hp, w)[7:8, :], 0.0)
        pn = jnp.where(s < ns - 1, _bdot(hx, w)[0:1, :], 0.0)
        p_prev = jnp.where(row == 0, pp, pltpu.roll(p, 1, 0))
        p_next = jnp.where(row == tm - 1, pn, pltpu.roll(p, tm - 1, 0))
        cw = cw_ref[:, cols]
        outs.append(cw[0:1] * p_prev + cw[1:2] * p + cw[2:3] * p_next
                    + cb_ref[:, cols])
    x0_ref[0] = outs[0]
    u_ref[0] = outs[2] * outs[1]


def _hy_in(x, g, sh, sc, w_bf, conv_w, conv_b, tm=512):
    b, s, d = x.shape
    tm = min(tm, s)
    hb = tm // 8
    nhb = s // 8
    kern = functools.partial(_hy_in_kernel, tm=tm, d=d)
    return pl.pallas_call(
        kern,
        out_shape=(jax.ShapeDtypeStruct((b, s, d), F32),
                   jax.ShapeDtypeStruct((b, s, d), F32)),
        grid=(b, s // tm),
        in_specs=[
            pl.BlockSpec((1, tm, d), lambda i, j: (i, j, 0)),
            pl.BlockSpec((1, 8, d), lambda i, j: (i, jnp.maximum(j * hb - 1, 0), 0)),
            pl.BlockSpec((1, 8, d), lambda i, j: (i, jnp.minimum((j + 1) * hb, nhb - 1), 0)),
            pl.BlockSpec((1, d), lambda i, j: (0, 0)),
            pl.BlockSpec((1, 1, d), lambda i, j: (i, 0, 0)),
            pl.BlockSpec((1, 1, d), lambda i, j: (i, 0, 0)),
            pl.BlockSpec((d, 3 * d), lambda i, j: (0, 0)),
            pl.BlockSpec((SHORT_CONV, 3 * d), lambda i, j: (0, 0)),
            pl.BlockSpec((1, 3 * d), lambda i, j: (0, 0)),
        ],
        out_specs=(pl.BlockSpec((1, tm, d), lambda i, j: (i, j, 0)),
                   pl.BlockSpec((1, tm, d), lambda i, j: (i, j, 0))),
        compiler_params=_params(("parallel", "parallel"), 56),
        name="hyena_in",
    )(x, x, x, g, sh, sc, w_bf, conv_w, conv_b)


FILT_T_COL = 127
FILT_VALID_COL = 126


def _hy_filter_kernel(z_ref, w1_ref, b1_ref, w2_ref, b2_ref, w3_ref, fr_ref,
                      dl_ref, gt_ref, nrm_ref):
    xb = pl.program_id(0)
    z = z_ref[...]
    fr = fr_ref[...]
    h = jnp.sin(fr * (_hdot(z, w1_ref[...]) + b1_ref[...]))
    h = jnp.sin(fr * (_hdot(h, w2_ref[...]) + b2_ref[...]))
    h = _hdot(h, w3_ref[0])
    t = z[:, FILT_T_COL:FILT_T_COL + 1]
    valid = z[:, FILT_VALID_COL:FILT_VALID_COL + 1]
    h = h * jnp.exp(-t * dl_ref[...]) * valid
    ht = h.T
    gt_ref[...] = ht
    part = jnp.sum(jnp.abs(ht), axis=1, keepdims=True)

    @pl.when(xb == 0)
    def _():
        nrm_ref[...] = jnp.zeros_like(nrm_ref)

    nrm_ref[...] += jnp.broadcast_to(part, nrm_ref.shape)


def _hy_filter(seq, w1, b1, w2, b2, w3, freq, xt=512):
    L = seq
    d = w3.shape[-1] // 2
    hid = w2.shape[0]
    emb = w1.shape[0]
    xt = min(xt, L)
    t = jnp.linspace(0.0, 1.0, L, dtype=F32)
    bands = jnp.linspace(1e-4, FILTER_BANDS - 1, FILTER_BANDS, dtype=F32)
    ang = (2.0 * math.pi / L) * jnp.arange(L, dtype=F32)[:, None] * bands[None, :]
    zf = jnp.concatenate([t[:, None], jnp.cos(ang), -jnp.sin(ang)], axis=-1)
    lag = jnp.abs(jnp.arange(2 * L) - L)
    valid = (lag < L).astype(F32)
    lagc = jnp.minimum(lag, L - 1)
    z2 = jnp.zeros((2 * L, LANES), F32)
    z2 = z2.at[:, :emb].set(zf[lagc])
    z2 = z2.at[:, FILT_T_COL].set(t[lagc])
    z2 = z2.at[:, FILT_VALID_COL].set(valid)
    w1p = jnp.zeros((LANES, hid), F32).at[:emb].set(w1)
    w3d = w3.reshape(hid, 2, d).transpose(1, 0, 2)
    deltas = jnp.abs(jnp.linspace(MIN_DECAY, MAX_DECAY, d, dtype=F32))[None, :]
    nxb = 2 * L // xt
    half = nxb // 2
    return pl.pallas_call(
        _hy_filter_kernel,
        out_shape=(jax.ShapeDtypeStruct((d, 2 * L), F32),
                   jax.ShapeDtypeStruct((d, LANES), F32)),
        grid=(nxb,),
        in_specs=[
            pl.BlockSpec((xt, LANES), lambda i: (i, 0)),
            pl.BlockSpec((LANES, hid), lambda i: (0, 0)),
            pl.BlockSpec((1, hid), lambda i: (0, 0)),
            pl.BlockSpec((hid, hid), lambda i: (0, 0)),
            pl.BlockSpec((1, hid), lambda i: (0, 0)),
            pl.BlockSpec((1, hid, d), lambda i: (jnp.where(i >= half, 0, 1), 0, 0)),
            pl.BlockSpec((1, hid), lambda i: (0, 0)),
            pl.BlockSpec((1, d), lambda i: (0, 0)),
        ],
        out_specs=(pl.BlockSpec((d, xt), lambda i: (0, i)),
                   pl.BlockSpec((d, LANES), lambda i: (0, 0))),
        compiler_params=_params(("arbitrary",), 32),
        name="hyena_filter",
    )(z2, w1p, b1[None, :], w2, b2[None, :], w3d, freq[None, :], deltas)


def _hy_conv_kernel(g_ref, nrm_ref, u_ref, y_ref, pg_ref, strip_ref, ucat_ref,
                    *, nb, bsz):
    P = CONV_BLOCK
    nshift = 2 * nb - 1
    nwin = 2 * nb * P // LANES
    ngrp = nshift * P // LANES
    zoff = nshift * P // LANES
    HR = LANES // 2

    @pl.when(pl.program_id(0) == 0)
    def _():
        ucat_ref[...] = jnp.zeros_like(ucat_ref)

    g = g_ref[0]
    gb = pltpu.bitcast(g.astype(BF16).astype(F32), I32)
    gm1 = pltpu.bitcast(pltpu.roll(g, 1, 1).astype(BF16).astype(F32), I32)
    pg_ref[...] = jnp.bitwise_or(jnp.bitwise_and(jnp.right_shift(gb, 16), 0xFFFF),
                                 jnp.bitwise_and(gm1, -65536))

    def rolled(w):
        pw = pg_ref[:, w * LANES:(w + 1) * LANES]
        return pltpu.roll(jnp.broadcast_to(pw, (HR, LANES)), 0, 1, stride=2, stride_axis=0)

    row = lax.broadcasted_iota(I32, (HR, LANES), 0)
    col = lax.broadcasted_iota(I32, (HR, LANES), 1)
    upper = col >= 2 * row

    nq = P // LANES
    cur = rolled(nwin - 1)
    for w in range(nwin - 1, 0, -1):
        prev = rolled(w - 1)
        blk = pltpu.bitcast(jnp.where(upper, cur, prev), BF16)
        for q in range(nq):
            m = zoff + q - w
            if 0 <= m < ngrp:
                strip_ref[m * LANES:(m + 1) * LANES, q * LANES:(q + 1) * LANES] = blk
        cur = prev

    u = u_ref[0].astype(BF16)
    rows = nb * bsz
    for m in range(nshift):
        dshift = (nb - 1 - m) * bsz
        if dshift >= 0:
            ucat_ref[dshift:rows, m * P:(m + 1) * P] = u[0:rows - dshift]
        else:
            ucat_ref[0:rows + dshift, m * P:(m + 1) * P] = u[-dshift:rows]
    y = _bdot(ucat_ref[...], strip_ref[...])
    y_ref[0] = y * (1.0 / nrm_ref[0, :, 0:1])


def _hy_conv(gt, nrm, ut, nb, bsz):
    d = ut.shape[0]
    P = CONV_BLOCK
    rows = nb * bsz
    nshift = 2 * nb - 1
    kern = functools.partial(_hy_conv_kernel, nb=nb, bsz=bsz)
    return pl.pallas_call(
        kern,
        out_shape=jax.ShapeDtypeStruct((d, rows, P), F32),
        grid=(d,),
        in_specs=[
            pl.BlockSpec((1, 1, 2 * nb * P), lambda i: (i, 0, 0)),
            pl.BlockSpec((1, 1, LANES), lambda i: (i, 0, 0)),
            pl.BlockSpec((1, rows, P), lambda i: (i, 0, 0)),
        ],
        out_specs=pl.BlockSpec((1, rows, P), lambda i: (i, 0, 0)),
        scratch_shapes=[
            pltpu.VMEM((1, 2 * nb * P), I32),
            pltpu.VMEM((nshift * P, P), BF16),
            pltpu.VMEM((rows, nshift * P), BF16),
        ],
        compiler_params=_params(("arbitrary",), 48),
        name="hyena_conv",
    )(gt.reshape(d, 1, 2 * nb * P), nrm.reshape(d, 1, LANES), ut)


def _mixer_tail(a_bf, w_ref, x_ref, gt_ref, g2_ref, sh2_ref, sc2_ref, wr_ref,
                xo_ref, hn_ref, lg_ref):
    xn = x_ref[0] + gt_ref[0] * _bdot(a_bf, w_ref[...])
    xo_ref[0] = xn
    hn = _norm_mod(xn, g2_ref[...], sc2_ref[0], sh2_ref[0])
    hn_ref[0] = hn.astype(BF16)
    lg_ref[0] = _hdot(hn, wr_ref[...])


def _hy_out_kernel(y_ref, u_ref, x0_ref, skip_ref, w_ref, x_ref, gt_ref, g2_ref,
                   sh2_ref, sc2_ref, wr_ref, xo_ref, hn_ref, lg_ref):
    a = (y_ref[0] + u_ref[0] * skip_ref[...]) * x0_ref[0]
    _mixer_tail(a.astype(BF16), w_ref, x_ref, gt_ref, g2_ref, sh2_ref, sc2_ref,
                wr_ref, xo_ref, hn_ref, lg_ref)


def _tail_specs(b, s, d, tm):
    tok = pl.BlockSpec((1, tm, d), lambda i, j: (i, j, 0))
    vec = pl.BlockSpec((1, d), lambda i, j: (0, 0))
    bvec = pl.BlockSpec((1, 1, d), lambda i, j: (i, 0, 0))
    wsq = pl.BlockSpec((d, d), lambda i, j: (0, 0))
    wr = pl.BlockSpec((d, LANES), lambda i, j: (0, 0))
    out_shape = (jax.ShapeDtypeStruct((b, s, d), F32),
                 jax.ShapeDtypeStruct((b, s, d), BF16),
                 jax.ShapeDtypeStruct((b, s, LANES), F32))
    out_specs = (tok, tok, pl.BlockSpec((1, tm, LANES), lambda i, j: (i, j, 0)))
    return tok, vec, bvec, wsq, wr, out_shape, out_specs


def _hy_out(y, u, x0, skip, w_bf, x, gt1, g2, sh2, sc2, wr, tm=512):
    b, s, d = x.shape
    tm = min(tm, s)
    tok, vec, bvec, wsq, wrs, out_shape, out_specs = _tail_specs(b, s, d, tm)
    return pl.pallas_call(
        _hy_out_kernel,
        out_shape=out_shape,
        grid=(b, s // tm),
        in_specs=[tok, tok, tok, vec, wsq, tok, bvec, vec, bvec, bvec, wrs],
        out_specs=out_specs,
        compiler_params=_params(("parallel", "parallel"), 48),
        name="hyena_out",
    )(y, u, x0, skip, w_bf, x, gt1, g2, sh2, sc2, wr)


def _gla_in_kernel(x_ref, g_ref, sh_ref, sc_ref, w_ref, wl_ref, wg_ref, bg_ref,
                   q_ref, k_ref, v_ref, r_ref, gf_ref, gb_ref, *, dk, dv, qscale):
    hn = _norm_mod(x_ref[0], g_ref[...], sc_ref[0], sh_ref[0]).astype(BF16)
    q_ref[0] = _bdot(hn, w_ref[:, 0:dk]) * qscale
    k_ref[0] = _bdot(hn, w_ref[:, dk:2 * dk])
    v_ref[0] = _bdot(hn, w_ref[:, 2 * dk:2 * dk + dv])
    r_ref[0] = _bdot(hn, w_ref[:, 2 * dk + dv:2 * dk + 2 * dv])
    low = _bdot(hn, wl_ref[...])
    gates = _log_sigmoid(_hdot(low, wg_ref[...]) + bg_ref[...]) * (1.0 / GLA_TAU)
    gf_ref[0] = gates[:, 0:dk]
    gb_ref[0] = gates[:, dk:2 * dk]


def _gla_in(x, g, sh, sc, w_main_bf, w_low_bf, w_gate, b_gate, dk, dv, tm=512):
    b, s, d = x.shape
    tm = min(tm, s)
    qscale = float((dk // GLA_HEADS) ** -0.5)
    kern = functools.partial(_gla_in_kernel, dk=dk, dv=dv, qscale=qscale)
    tokk = pl.BlockSpec((1, tm, dk), lambda i, j: (i, j, 0))
    tokv = pl.BlockSpec((1, tm, dv), lambda i, j: (i, j, 0))
    return pl.pallas_call(
        kern,
        out_shape=(jax.ShapeDtypeStruct((b, s, dk), F32),
                   jax.ShapeDtypeStruct((b, s, dk), F32),
                   jax.ShapeDtypeStruct((b, s, dv), F32),
                   jax.ShapeDtypeStruct((b, s, dv), F32),
                   jax.ShapeDtypeStruct((b, s, dk), F32),
                   jax.ShapeDtypeStruct((b, s, dk), F32)),
        grid=(b, s // tm),
        in_specs=[
            pl.BlockSpec((1, tm, d), lambda i, j: (i, j, 0)),
            pl.BlockSpec((1, d), lambda i, j: (0, 0)),
            pl.BlockSpec((1, 1, d), lambda i, j: (i, 0, 0)),
            pl.BlockSpec((1, 1, d), lambda i, j: (i, 0, 0)),
            pl.BlockSpec((d, 2 * dk + 2 * dv), lambda i, j: (0, 0)),
            pl.BlockSpec((d, LANES), lambda i, j: (0, 0)),
            pl.BlockSpec((LANES, 2 * dk), lambda i, j: (0, 0)),
            pl.BlockSpec((1, 2 * dk), lambda i, j: (0, 0)),
        ],
        out_specs=(tokk, tokk, tokv, tokv, tokk, tokk),
        compiler_params=_params(("parallel", "parallel"), 56),
        name="gla_in",
    )(x, g, sh, sc, w_main_bf, w_low_bf, w_gate, b_gate)


def _gla_core_kernel(q_ref, k_ref, v_ref, gf_ref, gb_ref, o_ref, sf_ref, sb_ref,
                     *, ngroup):
    C = GLA_CHUNK
    G = GLA_GROUP
    R = C * G
    ri = lax.broadcasted_iota(I32, (C, C), 0)
    ci = lax.broadcasted_iota(I32, (C, C), 1)
    mask_f = ci <= ri
    mask_b = ci > ri
    rowc = lax.broadcasted_iota(I32, (R, 1), 0) % C
    sf_ref[...] = jnp.zeros_like(sf_ref)
    sb_ref[...] = jnp.zeros_like(sb_ref)
    nt = (((1,), (1,)), ((), ()))
    tn = (((0,), (0,)), ((), ()))

    def chunk_cumsum(x, reverse):
        sh = 1
        while sh < C:
            if reverse:
                x = x + jnp.where(rowc < C - sh, pltpu.roll(x, R - sh, 0), 0.0)
            else:
                x = x + jnp.where(rowc >= sh, pltpu.roll(x, sh, 0), 0.0)
            sh *= 2
        return x

    def group(gi, g_ref, s_ref, reverse, accumulate):
        rows = pl.ds(pl.multiple_of(gi * R, R), R)
        q = q_ref[0, rows, :]
        k = k_ref[0, rows, :]
        v = v_ref[0, rows, :].astype(BF16)
        cum = chunk_cumsum(g_ref[0, rows, :], reverse)
        q_dec = (q * jnp.exp(cum)).astype(BF16)
        k_inv = (k * jnp.exp(-cum)).astype(BF16)
        st = s_ref[...]
        outs = [None] * G
        for n in (range(G - 1, -1, -1) if reverse else range(G)):
            sl = slice(n * C, (n + 1) * C)
            edge = n * C if reverse else n * C + C - 1
            tot = cum[edge:edge + 1, :]
            k_end = (k[sl] * jnp.exp(tot - cum[sl])).astype(BF16)
            att = lax.dot_general(q_dec[sl], k_inv[sl], nt, preferred_element_type=F32)
            att = jnp.where(mask_b if reverse else mask_f, att, 0.0).astype(BF16)
            outs[n] = _bdot(att, v[sl]) + lax.dot_general(
                q_dec[sl], st.astype(BF16), nt, preferred_element_type=F32)
            st = st * jnp.exp(tot) + lax.dot_general(
                v[sl], k_end, tn, preferred_element_type=F32)
        s_ref[...] = st
        o = jnp.concatenate(outs, axis=0)
        if accumulate:
            o_ref[0, rows, :] += o
        else:
            o_ref[0, rows, :] = o

    def first_half(i, carry):
        group(i, gf_ref, sf_ref, False, False)
        group(ngroup - 1 - i, gb_ref, sb_ref, True, False)
        return carry

    def second_half(i, carry):
        group(i, gf_ref, sf_ref, False, True)
        group(ngroup - 1 - i, gb_ref, sb_ref, True, True)
        return carry

    half = ngroup // 2
    lax.fori_loop(0, half, first_half, 0)
    lax.fori_loop(half, ngroup, second_half, 0)


def _gla_core(q, k, v, gf, gb):
    b, s, dk = q.shape
    dv = v.shape[-1]
    hk = dk // GLA_HEADS
    hv = dv // GLA_HEADS
    ngroup = s // (GLA_CHUNK * GLA_GROUP)
    assert ngroup % 2 == 0
    kern = functools.partial(_gla_core_kernel, ngroup=ngroup)
    speck = pl.BlockSpec((1, s, hk), lambda i, h: (i, 0, h))
    specv = pl.BlockSpec((1, s, hv), lambda i, h: (i, 0, h))
    return pl.pallas_call(
        kern,
        out_shape=jax.ShapeDtypeStruct((b, s, dv), F32),
        grid=(b, GLA_HEADS),
        in_specs=[speck, speck, specv, speck, speck],
        out_specs=specv,
        scratch_shapes=[pltpu.VMEM((hv, hk), F32), pltpu.VMEM((hv, hk), F32)],
        compiler_params=_params(("parallel", "parallel"), 48),
        name="gla_core",
    )(q, k, v, gf, gb)


def _gla_out_kernel(o_ref, r_ref, ng_ref, w_ref, x_ref, gt_ref, g2_ref, sh2_ref,
                    sc2_ref, wr_ref, xo_ref, hn_ref, lg_ref, *, hv):
    o = o_ref[0]
    parts = []
    for h in range(GLA_HEADS):
        oh = o[:, h * hv:(h + 1) * hv]
        ms = jnp.mean(oh * oh, axis=-1, keepdims=True)
        parts.append(oh * lax.rsqrt(ms + EPS))
    on = jnp.concatenate(parts, axis=-1) * ng_ref[...]
    a = on * _silu(r_ref[0])
    _mixer_tail(a.astype(BF16), w_ref, x_ref, gt_ref, g2_ref, sh2_ref, sc2_ref,
                wr_ref, xo_ref, hn_ref, lg_ref)


def _gla_out(o, r, ng, w_bf, x, gt1, g2, sh2, sc2, wr, tm=512):
    b, s, d = x.shape
    dv = o.shape[-1]
    tm = min(tm, s)
    tok, vec, bvec, wsq, wrs, out_shape, out_specs = _tail_specs(b, s, d, tm)
    tokv = pl.BlockSpec((1, tm, dv), lambda i, j: (i, j, 0))
    kern = functools.partial(_gla_out_kernel, hv=dv // GLA_HEADS)
    return pl.pallas_call(
        kern,
        out_shape=out_shape,
        grid=(b, s // tm),
        in_specs=[tokv, tokv, pl.BlockSpec((1, dv), lambda i, j: (0, 0)),
                  pl.BlockSpec((dv, d), lambda i, j: (0, 0)),
                  tok, bvec, vec, bvec, bvec, wrs],
        out_specs=out_specs,
        compiler_params=_params(("parallel", "parallel"), 48),
        name="gla_out",
    )(o, r, ng, w_bf, x, gt1, g2, sh2, sc2, wr)


def _prefix_sum_lanes(x, n):
    lane = lax.broadcasted_iota(I32, x.shape, 1)
    sh = 1
    while sh < n:
        x = x + jnp.where(lane >= sh, pltpu.roll(x, sh, 1), 0)
        sh *= 2
    return x


def _route_kernel(lg_ref, slot_t_ref, slot_ref, aff_ref, toff_ref, *, cap, seq):
    E = N_EXPERTS
    lg = lg_ref[0]
    lane = lax.broadcasted_iota(I32, lg.shape, 1)
    live = lane < E
    lgm = jnp.where(live, lg, -jnp.inf)
    mx = jnp.max(lgm, axis=-1, keepdims=True)
    ex = jnp.where(live, jnp.exp(lgm - mx), 0.0)
    aff = ex / jnp.sum(ex, axis=-1, keepdims=True)
    aff_ref[0] = aff
    aff_t = aff.T[0:E, :]
    bits = pltpu.bitcast(aff_t, I32)

    def search(_, lohi):
        lo, hi = lohi
        mid = lo + (hi - lo + 1) // 2
        cnt = jnp.sum((bits >= mid).astype(I32), axis=1, keepdims=True)
        ok = cnt >= cap
        return jnp.where(ok, mid, lo), jnp.where(ok, hi, mid - 1)

    lo0 = jnp.zeros((E, 1), I32)
    hi0 = jnp.full((E, 1), 0x7F800000, I32)
    thr, _ = lax.fori_loop(0, 32, search, (lo0, hi0))
    gt = bits > thr
    eq = bits == thr
    need = cap - jnp.sum(gt.astype(I32), axis=1, keepdims=True)
    eq_i = eq.astype(I32)
    eq_rank = _prefix_sum_lanes(eq_i, seq) - eq_i
    sel = jnp.logical_or(gt, jnp.logical_and(eq, eq_rank < need))
    sel_i = sel.astype(I32)
    pos = _prefix_sum_lanes(sel_i, seq) - sel_i
    slot_t = jnp.where(sel, pos, -1)
    slot_t_ref[0] = slot_t
    pad = jnp.full((LANES - E, seq), -1, I32)
    slot_ref[0] = jnp.concatenate([slot_t, pad], axis=0).T
    lane_e = lax.broadcasted_iota(I32, (E, LANES), 1)
    run = jnp.zeros((E, 1), I32)
    toff = jnp.zeros((E, LANES), I32)
    for j in range(seq // MOE_TILE):
        run = run + jnp.sum(sel_i[:, j * MOE_TILE:(j + 1) * MOE_TILE], axis=1, keepdims=True)
        toff = toff + jnp.where(lane_e == j + 1, run, 0)
    toff_ref[0] = toff


def _route(logits, cap):
    b, s, _ = logits.shape
    assert s // MOE_TILE < LANES
    kern = functools.partial(_route_kernel, cap=cap, seq=s)
    return pl.pallas_call(
        kern,
        out_shape=(jax.ShapeDtypeStruct((b, N_EXPERTS, s), I32),
                   jax.ShapeDtypeStruct((b, s, LANES), I32),
                   jax.ShapeDtypeStruct((b, s, LANES), F32),
                   jax.ShapeDtypeStruct((b, N_EXPERTS, LANES), I32)),
        grid=(b,),
        in_specs=[pl.BlockSpec((1, s, LANES), lambda i: (i, 0, 0))],
        out_specs=(pl.BlockSpec((1, N_EXPERTS, s), lambda i: (i, 0, 0)),
                   pl.BlockSpec((1, s, LANES), lambda i: (i, 0, 0)),
                   pl.BlockSpec((1, s, LANES), lambda i: (i, 0, 0)),
                   pl.BlockSpec((1, N_EXPERTS, LANES), lambda i: (i, 0, 0))),
        compiler_params=_params(("parallel",), 48),
        name="moe_route",
    )(logits)


def _tile_range(toff_ref, base, nt, lo, hi):
    jlo = jnp.int32(0)
    jend = jnp.int32(0)
    for j in range(nt):
        jlo = jlo + jnp.where(toff_ref[base + j + 1] <= lo, 1, 0)
        jend = jend + jnp.where(toff_ref[base + j] < hi, 1, 0)
    return jlo, jend


def _moe_ffn_kernel(toff_ref, hn_ref, slot_ref, wg_ref, wu_ref, wd_ref, y_ref,
                    xg_ref, acc_ref, gacc_ref, *, cap, seq):
    f = pl.program_id(2)
    nf = pl.num_programs(2)

    @pl.when(f == 0)
    def _():
        base = (pl.program_id(0) * pl.num_programs(1) + pl.program_id(1)) * LANES
        nt = seq // MOE_TILE
        wt = min(MOE_WIN_TILES, nt)
        r = lax.broadcasted_iota(I32, (MOE_SLOT_BLOCK, 1), 0)

        def gathered(t0, width, r0):
            srow = slot_ref[0, 0, :, pl.ds(t0, width)]
            onehot = jnp.where(srow == r + r0, 1.0, 0.0).astype(BF16)
            return _bdot(onehot, hn_ref[0, pl.ds(t0, width), :])

        for kb in range(cap // MOE_SLOT_BLOCK):
            r0 = kb * MOE_SLOT_BLOCK
            jlo, jend = _tile_range(toff_ref, base, nt, r0, r0 + MOE_SLOT_BLOCK)
            jw = jnp.minimum(jlo, nt - wt)
            gacc_ref[...] = gathered(pl.multiple_of(jw * MOE_TILE, MOE_TILE),
                                     wt * MOE_TILE, r0)

            def tile(j, carry):
                gacc_ref[...] += gathered(pl.multiple_of(j * MOE_TILE, MOE_TILE),
                                          MOE_TILE, r0)
                return carry

            lax.fori_loop(jnp.maximum(jlo, jw + wt), jend, tile, 0)
            xg_ref[r0:r0 + MOE_SLOT_BLOCK, :] = gacc_ref[...].astype(BF16)
        acc_ref[...] = jnp.zeros_like(acc_ref)

    xg = xg_ref[...]
    a = _bdot(xg, wg_ref[0])
    u = _bdot(xg, wu_ref[0])
    hh = (_silu(a) * u).astype(BF16)
    acc_ref[...] += _bdot(hh, wd_ref[0])

    @pl.when(f == nf - 1)
    def _():
        y_ref[0, 0] = acc_ref[...].astype(BF16)


def _moe_ffn(hn_bf, slot_t, toff, wg_bf, wu_bf, wd_bf, cap, fc=512):
    b, s, d = hn_bf.shape
    e, _, fdim = wg_bf.shape
    fc = min(fc, fdim)
    assert cap % MOE_SLOT_BLOCK == 0 and s % MOE_TILE == 0
    kern = functools.partial(_moe_ffn_kernel, cap=cap, seq=s)
    grid_spec = pltpu.PrefetchScalarGridSpec(
        num_scalar_prefetch=1,
        grid=(b, e, fdim // fc),
        in_specs=[
            pl.BlockSpec((1, s, d), lambda i, j, f, t: (i, 0, 0)),
            pl.BlockSpec((1, 1, 1, s), lambda i, j, f, t: (i, j, 0, 0)),
            pl.BlockSpec((1, d, fc), lambda i, j, f, t: (j, 0, f)),
            pl.BlockSpec((1, d, fc), lambda i, j, f, t: (j, 0, f)),
            pl.BlockSpec((1, fc, d), lambda i, j, f, t: (j, f, 0)),
        ],
        out_specs=pl.BlockSpec((1, 1, cap, d), lambda i, j, f, t: (i, j, 0, 0)),
        scratch_shapes=[pltpu.VMEM((cap, d), BF16), pltpu.VMEM((cap, d), F32),
                        pltpu.VMEM((MOE_SLOT_BLOCK, d), F32)],
    )
    return pl.pallas_call(
        kern,
        out_shape=jax.ShapeDtypeStruct((b, e, cap, d), BF16),
        grid_spec=grid_spec,
        compiler_params=_params(("parallel", "arbitrary", "arbitrary"), 56),
        name="moe_ffn",
    )(toff.reshape(-1), hn_bf, slot_t.reshape(b, e, 1, s), wg_bf, wu_bf, wd_bf)


def _moe_combine_kernel(toff_ref, x_ref, gt_ref, slot_ref, aff_ref, y_ref, gfin_ref,
                        o_ref, acc_ref, *, cap, final, tt):
    e = pl.program_id(2)
    ne = pl.num_programs(2)
    nsub = tt // MOE_TILE
    kw = min(MOE_TILE, cap)

    @pl.when(e == 0)
    def _():
        acc_ref[...] = jnp.zeros_like(acc_ref)

    slot = slot_ref[0]
    lane = lax.broadcasted_iota(I32, slot.shape, 1)
    pick = lane == e
    slot_e = jnp.sum(jnp.where(pick, slot, 0), axis=1, keepdims=True)
    aff_e = jnp.sum(jnp.where(pick, aff_ref[0], 0.0), axis=1, keepdims=True)
    base = (pl.program_id(0) * ne + e) * LANES + pl.program_id(1) * nsub
    lo = toff_ref[base]
    hi = toff_ref[base + nsub]
    start = jnp.minimum((lo // BF16_ROWS) * BF16_ROWS, cap - kw)
    fits = hi <= start + kw

    @pl.when(fits)
    def _():
        s0 = pl.multiple_of(start, BF16_ROWS)
        r = lax.broadcasted_iota(I32, (1, kw), 1) + s0
        onehot = jnp.where(slot_e == r, 1.0, 0.0).astype(BF16)
        acc_ref[...] += aff_e * _bdot(onehot, y_ref[0, 0, pl.ds(s0, kw), :])

    @pl.when(jnp.logical_not(fits))
    def _():
        r = lax.broadcasted_iota(I32, (1, cap), 1)
        onehot = jnp.where(slot_e == r, 1.0, 0.0).astype(BF16)
        acc_ref[...] += aff_e * _bdot(onehot, y_ref[0, 0])

    @pl.when(e == ne - 1)
    def _():
        xn = x_ref[0] + gt_ref[0] * acc_ref[...]
        if final:
            ms = jnp.mean(xn * xn, axis=-1, keepdims=True)
            xn = xn * lax.rsqrt(ms + EPS) * gfin_ref[...]
        o_ref[0] = xn


def _moe_combine(x, gt2, slot, aff, toff, y, gfin, cap, final, tt=1024):
    b, s, d = x.shape
    e = y.shape[1]
    tt = min(tt, s)
    assert tt % MOE_TILE == 0 and cap % min(MOE_TILE, cap) == 0
    kern = functools.partial(_moe_combine_kernel, cap=cap, final=final, tt=tt)
    grid_spec = pltpu.PrefetchScalarGridSpec(
        num_scalar_prefetch=1,
        grid=(b, s // tt, e),
        in_specs=[
            pl.BlockSpec((1, tt, d), lambda i, t, j, o: (i, t, 0)),
            pl.BlockSpec((1, 1, d), lambda i, t, j, o: (i, 0, 0)),
            pl.BlockSpec((1, tt, LANES), lambda i, t, j, o: (i, t, 0)),
            pl.BlockSpec((1, tt, LANES), lambda i, t, j, o: (i, t, 0)),
            pl.BlockSpec((1, 1, cap, d), lambda i, t, j, o: (i, j, 0, 0)),
            pl.BlockSpec((1, d), lambda i, t, j, o: (0, 0)),
        ],
        out_specs=pl.BlockSpec((1, tt, d), lambda i, t, j, o: (i, t, 0)),
        scratch_shapes=[pltpu.VMEM((tt, d), F32)],
    )
    return pl.pallas_call(
        kern,
        out_shape=jax.ShapeDtypeStruct((b, s, d), F32),
        grid_spec=grid_spec,
        compiler_params=_params(("parallel", "parallel", "arbitrary"), 48),
        name="moe_combine",
    )(toff.reshape(-1), x, gt2, slot, aff, y, gfin)


def kernel(x, c, w_ada, b_ada, norm_mix, norm_ffn, hy_w_in, hy_conv_w, hy_conv_b, hy_f_w1, hy_f_b1, hy_f_w2, hy_f_b2, hy_f_w3, hy_f_freq, hy_skip, hy_w_out, gla_w_in, gla_w_gf, gla_b_gf, gla_w_gb, gla_b_gb, gla_norm, gla_w_out, moe_router, moe_w_gate, moe_w_up, moe_w_down, norm_final):
    b, s, d = x.shape
    depth = w_ada.shape[0]
    n_exp = moe_router.shape[-1]
    cap = EC_CAPACITY * s // n_exp
    dk = gla_w_gf.shape[-1]
    dv = gla_w_out.shape[1]
    nb = s // CONV_BLOCK

    mod = _adaln(c, w_ada, b_ada)
    for i in range(depth):
        j = i // 2
        sh1, sc1, gt1, sh2, sc2, gt2 = (mod[i, m].reshape(b, 1, d) for m in range(N_MOD))
        g1 = norm_mix[i][None, :]
        g2 = norm_ffn[i][None, :]
        wr = jnp.zeros((d, LANES), F32).at[:, :n_exp].set(moe_router[i])
        if i % 2 == 0:
            x0, u = _hy_in(x, g1, sh1, sc1, hy_w_in[j].astype(BF16), hy_conv_w[j],
                           hy_conv_b[j][None, :])
            gt, nrm = _hy_filter(s, hy_f_w1[j], hy_f_b1[j], hy_f_w2[j], hy_f_b2[j],
                                 hy_f_w3[j], hy_f_freq[j])
            ut = u.reshape(b, nb, CONV_BLOCK, d).transpose(3, 1, 0, 2)
            yt = _hy_conv(gt, nrm, ut.reshape(d, nb * b, CONV_BLOCK), nb, b)
            y = yt.reshape(d, nb, b, CONV_BLOCK).transpose(2, 1, 3, 0).reshape(b, s, d)
            x, hn, lg = _hy_out(y, u, x0, hy_skip[j][None, :], hy_w_out[j].astype(BF16),
                                x, gt1, g2, sh2, sc2, wr)
        else:
            w_in = gla_w_in[j]
            w_low = jnp.zeros((d, LANES), F32).at[:, :2 * GLA_GATE_RANK].set(
                w_in[:, 2 * dk + 2 * dv:]).astype(BF16)
            w_gate = jnp.zeros((LANES, 2 * dk), F32)
            w_gate = w_gate.at[:GLA_GATE_RANK, :dk].set(gla_w_gf[j])
            w_gate = w_gate.at[GLA_GATE_RANK:2 * GLA_GATE_RANK, dk:].set(gla_w_gb[j])
            b_gate = jnp.concatenate([gla_b_gf[j], gla_b_gb[j]])[None, :]
            q, k, v, r, gf, gb = _gla_in(x, g1, sh1, sc1,
                                         w_in[:, :2 * dk + 2 * dv].astype(BF16),
                                         w_low, w_gate, b_gate, dk, dv)
            o = _gla_core(q, k, v, gf, gb)
            x, hn, lg = _gla_out(o, r, gla_norm[j].reshape(1, dv),
                                 gla_w_out[j].astype(BF16), x, gt1, g2, sh2, sc2, wr)
        slot_t, slot, aff, toff = _route(lg, cap)
        y = _moe_ffn(hn, slot_t, toff, moe_w_gate[i].astype(BF16), moe_w_up[i].astype(BF16),
                     moe_w_down[i].astype(BF16), cap)
        x = _moe_combine(x, gt2, slot, aff, toff, y, norm_final[None, :], cap,
                         final=(i == depth - 1))
    return x
```

```python
import functools
import math

import jax
import jax.numpy as jnp
from jax import lax
from jax.experimental import pallas as pl
from jax.experimental.pallas import tpu as pltpu

F32 = jnp.float32
BF16 = jnp.bfloat16
I32 = jnp.int32
HIGHEST = lax.Precision.HIGHEST

EPS = 1e-6
N_MOD = 6
LANES = 128
VMEM_LIMIT_CAP = 56 << 20

SHORT_CONV = 3
FILTER_BANDS = 16
DECAY_TARGET = 1e-2
FAST_DECAY_PCT = 0.3
SLOW_DECAY_PCT = 1.5
MIN_DECAY = math.log(DECAY_TARGET) / SLOW_DECAY_PCT
MAX_DECAY = math.log(DECAY_TARGET) / FAST_DECAY_PCT
CONV_BLOCK = 256


GLA_HEADS = 4
GLA_GATE_RANK = 16
GLA_TAU = 16.0
GLA_CHUNK = 64
GLA_GROUP = 4

N_EXPERTS = 16
EC_CAPACITY = 2
MOE_TILE = 256
MOE_SLOT_BLOCK = 128
MOE_WIN_TILES = 6
BF16_ROWS = 16


def _params(sem, vmem_mb):
    return pltpu.CompilerParams(
        dimension_semantics=sem,
        vmem_limit_bytes=min(vmem_mb << 20, VMEM_LIMIT_CAP))


def _sigmoid(x):
    return 1.0 / (1.0 + jnp.exp(-x))


def _silu(x):
    return x * _sigmoid(x)


def _log_sigmoid(x):
    return jnp.minimum(x, 0.0) - jnp.log(1.0 + jnp.exp(-jnp.abs(x)))


def _norm_mod(x, g, sc, sh):
    ms = jnp.mean(x * x, axis=-1, keepdims=True)
    return (x * lax.rsqrt(ms + EPS) * g) * (1.0 + sc) + sh


def _bdot(a, b):
    return jnp.dot(a, b, preferred_element_type=F32)


def _hdot(a, b):
    return jnp.dot(a, b, preferred_element_type=F32, precision=HIGHEST)


def _split_bf16(a):
    hi = a.astype(BF16)
    return hi, (a - hi.astype(F32)).astype(BF16)


def _dot3(a, b_hi, b_lo):
    a_hi, a_lo = _split_bf16(a)
    return _bdot(a_hi, b_hi) + _bdot(a_hi, b_lo) + _bdot(a_lo, b_hi)


def _adaln_kernel(c_ref, w_ref, b_ref, o_ref):
    cond = _silu(c_ref[...])
    o_ref[0, 0] = _hdot(cond, w_ref[0]) + b_ref[0]


def _adaln(c, w_ada, b_ada):
    depth, d, _ = w_ada.shape
    b = c.shape[0]
    return pl.pallas_call(
        _adaln_kernel,
        out_shape=jax.ShapeDtypeStruct((depth, N_MOD, b, d), F32),
        grid=(depth, N_MOD),
        in_specs=[
            pl.BlockSpec((b, d), lambda i, n: (0, 0)),
            pl.BlockSpec((1, d, d), lambda i, n: (i, 0, n)),
            pl.BlockSpec((1, 1, d), lambda i, n: (i, 0, n)),
        ],
        out_specs=pl.BlockSpec((1, 1, b, d), lambda i, n: (i, n, 0, 0)),
        compiler_params=_params(("parallel", "parallel"), 32),
        name="adaln_mod",
    )(c, w_ada, b_ada.reshape(depth, 1, N_MOD * d))


def _hy_in_kernel(x_ref, xp_ref, xn_ref, g_ref, sh_ref, sc_ref, w_ref, cw_ref,
                  cb_ref, x0_ref, u_ref, *, tm, d):
    s = pl.program_id(1)
    ns = pl.num_programs(1)
    g = g_ref[...]
    sh = sh_ref[0]
    sc = sc_ref[0]
    hn = _norm_mod(x_ref[0], g, sc, sh).astype(BF16)
    hp = _norm_mod(xp_ref[0], g, sc, sh).astype(BF16)
    hx = _norm_mod(xn_ref[0], g, sc, sh).astype(BF16)
    row = lax.broadcasted_iota(I32, (tm, 1), 0)
    outs = []
    for part in range(3):
        cols = slice(part * d, (part + 1) * d)
        w = w_ref[:, cols]
        p = _bdot(hn, w)
        pp = jnp.where(s > 0, _bdot(hp, w)[7:8, :], 0.0)
        pn = jnp.where(s < ns - 1, _bdot(hx, w)[0:1, :], 0.0)
        p_prev = jnp.where(row == 0, pp, pltpu.roll(p, 1, 0))
        p_next = jnp.where(row == tm - 1, pn, pltpu.roll(p, tm - 1, 0))
        cw = cw_ref[:, cols]
        outs.append(cw[0:1] * p_prev + cw[1:2] * p + cw[2:3] * p_next
                    + cb_ref[:, cols])
    x0_ref[0] = outs[0]
    u_ref[0] = outs[2] * outs[1]


def _hy_in(x, g, sh, sc, w_bf, conv_w, conv_b, tm=512):
    b, s, d = x.shape
    tm = min(tm, s)
    hb = tm // 8
    nhb = s // 8
    kern = functools.partial(_hy_in_kernel, tm=tm, d=d)
    return pl.pallas_call(
        kern,
        out_shape=(jax.ShapeDtypeStruct((b, s, d), F32),
                   jax.ShapeDtypeStruct((b, s, d), F32)),
        grid=(b, s // tm),
        in_specs=[
            pl.BlockSpec((1, tm, d), lambda i, j: (i, j, 0)),
            pl.BlockSpec((1, 8, d), lambda i, j: (i, jnp.maximum(j * hb - 1, 0), 0)),
            pl.BlockSpec((1, 8, d), lambda i, j: (i, jnp.minimum((j + 1) * hb, nhb - 1), 0)),
            pl.BlockSpec((1, d), lambda i, j: (0, 0)),
            pl.BlockSpec((1, 1, d), lambda i, j: (i, 0, 0)),
            pl.BlockSpec((1, 1, d), lambda i, j: (i, 0, 0)),
            pl.BlockSpec((d, 3 * d), lambda i, j: (0, 0)),
            pl.BlockSpec((SHORT_CONV, 3 * d), lambda i, j: (0, 0)),
            pl.BlockSpec((1, 3 * d), lambda i, j: (0, 0)),
        ],
        out_specs=(pl.BlockSpec((1, tm, d), lambda i, j: (i, j, 0)),
                   pl.BlockSpec((1, tm, d), lambda i, j: (i, j, 0))),
        compiler_params=_params(("parallel", "parallel"), 56),
        name="hyena_in",
    )(x, x, x, g, sh, sc, w_bf, conv_w, conv_b)


FILT_T_COL = 127
FILT_VALID_COL = 126


def _hy_filter_kernel(z_ref, w1_ref, b1_ref, w2_ref, b2_ref, w3_ref, fr_ref,
                      dl_ref, gt_ref, nrm_ref):
    xb = pl.program_id(0)
    z = z_ref[...]
    fr = fr_ref[...]
    h = jnp.sin(fr * (_hdot(z, w1_ref[...]) + b1_ref[...]))
    h = jnp.sin(fr * (_hdot(h, w2_ref[...]) + b2_ref[...]))
    h = _hdot(h, w3_ref[0])
    t = z[:, FILT_T_COL:FILT_T_COL + 1]
    valid = z[:, FILT_VALID_COL:FILT_VALID_COL + 1]
    h = h * jnp.exp(-t * dl_ref[...]) * valid
    ht = h.T
    gt_ref[...] = ht
    part = jnp.sum(jnp.abs(ht), axis=1, keepdims=True)

    @pl.when(xb == 0)
    def _():
        nrm_ref[...] = jnp.zeros_like(nrm_ref)

    nrm_ref[...] += jnp.broadcast_to(part, nrm_ref.shape)


def _hy_filter(seq, w1, b1, w2, b2, w3, freq, xt=512):
    L = seq
    d = w3.shape[-1] // 2
    hid = w2.shape[0]
    emb = w1.shape[0]
    xt = min(xt, L)
    t = jnp.linspace(0.0, 1.0, L, dtype=F32)
    bands = jnp.linspace(1e-4, FILTER_BANDS - 1, FILTER_BANDS, dtype=F32)
    ang = (2.0 * math.pi / L) * jnp.arange(L, dtype=F32)[:, None] * bands[None, :]
    zf = jnp.concatenate([t[:, None], jnp.cos(ang), -jnp.sin(ang)], axis=-1)
    lag = jnp.abs(jnp.arange(2 * L) - L)
    valid = (lag < L).astype(F32)
    lagc = jnp.minimum(lag, L - 1)
    z2 = jnp.zeros((2 * L, LANES), F32)
    z2 = z2.at[:, :emb].set(zf[lagc])
    z2 = z2.at[:, FILT_T_COL].set(t[lagc])
    z2 = z2.at[:, FILT_VALID_COL].set(valid)
    w1p = jnp.zeros((LANES, hid), F32).at[:emb].set(w1)
    w3d = w3.reshape(hid, 2, d).transpose(1, 0, 2)
    deltas = jnp.abs(jnp.linspace(MIN_DECAY, MAX_DECAY, d, dtype=F32))[None, :]
    nxb = 2 * L // xt
    half = nxb // 2
    return pl.pallas_call(
        _hy_filter_kernel,
        out_shape=(jax.ShapeDtypeStruct((d, 2 * L), F32),
                   jax.ShapeDtypeStruct((d, LANES), F32)),
        grid=(nxb,),
        in_specs=[
            pl.BlockSpec((xt, LANES), lambda i: (i, 0)),
            pl.BlockSpec((LANES, hid), lambda i: (0, 0)),
            pl.BlockSpec((1, hid), lambda i: (0, 0)),
            pl.BlockSpec((hid, hid), lambda i: (0, 0)),
            pl.BlockSpec((1, hid), lambda i: (0, 0)),
            pl.BlockSpec((1, hid, d), lambda i: (jnp.where(i >= half, 0, 1), 0, 0)),
            pl.BlockSpec((1, hid), lambda i: (0, 0)),
            pl.BlockSpec((1, d), lambda i: (0, 0)),
        ],
        out_specs=(pl.BlockSpec((d, xt), lambda i: (0, i)),
                   pl.BlockSpec((d, LANES), lambda i: (0, 0))),
        compiler_params=_params(("arbitrary",), 32),
        name="hyena_filter",
    )(z2, w1p, b1[None, :], w2, b2[None, :], w3d, freq[None, :], deltas)


def _hy_conv_kernel(g_ref, nrm_ref, u_ref, y_ref, pg_ref, strip_ref, ucat_ref,
                    *, nb, bsz):
    P = CONV_BLOCK
    nshift = 2 * nb - 1
    nwin = 2 * nb * P // LANES
    ngrp = nshift * P // LANES
    zoff = nshift * P // LANES
    HR = LANES // 2

    @pl.when(pl.program_id(0) == 0)
    def _():
        ucat_ref[...] = jnp.zeros_like(ucat_ref)

    g = g_ref[0]
    gb = pltpu.bitcast(g.astype(BF16).astype(F32), I32)
    gm1 = pltpu.bitcast(pltpu.roll(g, 1, 1).astype(BF16).astype(F32), I32)
    pg_ref[...] = jnp.bitwise_or(jnp.bitwise_and(jnp.right_shift(gb, 16), 0xFFFF),
                                 jnp.bitwise_and(gm1, -65536))

    def rolled(w):
        pw = pg_ref[:, w * LANES:(w + 1) * LANES]
        return pltpu.roll(jnp.broadcast_to(pw, (HR, LANES)), 0, 1, stride=2, stride_axis=0)

    row = lax.broadcasted_iota(I32, (HR, LANES), 0)
    col = lax.broadcasted_iota(I32, (HR, LANES), 1)
    upper = col >= 2 * row

    nq = P // LANES
    cur = rolled(nwin - 1)
    for w in range(nwin - 1, 0, -1):
        prev = rolled(w - 1)
        blk = pltpu.bitcast(jnp.where(upper, cur, prev), BF16)
        for q in range(nq):
            m = zoff + q - w
            if 0 <= m < ngrp:
                strip_ref[m * LANES:(m + 1) * LANES, q * LANES:(q + 1) * LANES] = blk
        cur = prev

    u = u_ref[0].astype(BF16)
    rows = nb * bsz
    for m in range(nshift):
        dshift = (nb - 1 - m) * bsz
        if dshift >= 0:
            ucat_ref[dshift:rows, m * P:(m + 1) * P] = u[0:rows - dshift]
        else:
            ucat_ref[0:rows + dshift, m * P:(m + 1) * P] = u[-dshift:rows]
    y = _bdot(ucat_ref[...], strip_ref[...])
    y_ref[0] = y * (1.0 / nrm_ref[0, :, 0:1])


def _hy_conv(gt, nrm, ut, nb, bsz):
    d = ut.shape[0]
    P = CONV_BLOCK
    rows = nb * bsz
    nshift = 2 * nb - 1
    kern = functools.partial(_hy_conv_kernel, nb=nb, bsz=bsz)
    return pl.pallas_call(
        kern,
        out_shape=jax.ShapeDtypeStruct((d, rows, P), F32),
        grid=(d,),
        in_specs=[
            pl.BlockSpec((1, 1, 2 * nb * P), lambda i: (i, 0, 0)),
            pl.BlockSpec((1, 1, LANES), lambda i: (i, 0, 0)),
            pl.BlockSpec((1, rows, P), lambda i: (i, 0, 0)),
        ],
        out_specs=pl.BlockSpec((1, rows, P), lambda i: (i, 0, 0)),
        scratch_shapes=[
            pltpu.VMEM((1, 2 * nb * P), I32),
            pltpu.VMEM((nshift * P, P), BF16),
            pltpu.VMEM((rows, nshift * P), BF16),
        ],
        compiler_params=_params(("arbitrary",), 48),
        name="hyena_conv",
    )(gt.reshape(d, 1, 2 * nb * P), nrm.reshape(d, 1, LANES), ut)


def _mixer_tail(a_bf, w_ref, x_ref, gt_ref, g2_ref, sh2_ref, sc2_ref, wr_ref,
                xo_ref, hn_ref, lg_ref):
    xn = x_ref[0] + gt_ref[0] * _bdot(a_bf, w_ref[...])
    xo_ref[0] = xn
    hn = _norm_mod(xn, g2_ref[...], sc2_ref[0], sh2_ref[0])
    hn_ref[0] = hn.astype(BF16)
    lg_ref[0] = _dot3(hn, wr_ref[0], wr_ref[1])


def _hy_out_kernel(y_ref, u_ref, x0_ref, skip_ref, w_ref, x_ref, gt_ref, g2_ref,
                   sh2_ref, sc2_ref, wr_ref, xo_ref, hn_ref, lg_ref):
    a = (y_ref[0] + u_ref[0] * skip_ref[...]) * x0_ref[0]
    _mixer_tail(a.astype(BF16), w_ref, x_ref, gt_ref, g2_ref, sh2_ref, sc2_ref,
                wr_ref, xo_ref, hn_ref, lg_ref)


def _tail_specs(b, s, d, tm):
    tok = pl.BlockSpec((1, tm, d), lambda i, j: (i, j, 0))
    vec = pl.BlockSpec((1, d), lambda i, j: (0, 0))
    bvec = pl.BlockSpec((1, 1, d), lambda i, j: (i, 0, 0))
    wsq = pl.BlockSpec((d, d), lambda i, j: (0, 0))
    wr = pl.BlockSpec((2, d, LANES), lambda i, j: (0, 0, 0))
    out_shape = (jax.ShapeDtypeStruct((b, s, d), F32),
                 jax.ShapeDtypeStruct((b, s, d), BF16),
                 jax.ShapeDtypeStruct((b, s, LANES), F32))
    out_specs = (tok, tok, pl.BlockSpec((1, tm, LANES), lambda i, j: (i, j, 0)))
    return tok, vec, bvec, wsq, wr, out_shape, out_specs


def _hy_out(y, u, x0, skip, w_bf, x, gt1, g2, sh2, sc2, wr, tm=512):
    b, s, d = x.shape
    tm = min(tm, s)
    tok, vec, bvec, wsq, wrs, out_shape, out_specs = _tail_specs(b, s, d, tm)
    return pl.pallas_call(
        _hy_out_kernel,
        out_shape=out_shape,
        grid=(b, s // tm),
        in_specs=[tok, tok, tok, vec, wsq, tok, bvec, vec, bvec, bvec, wrs],
        out_specs=out_specs,
        compiler_params=_params(("parallel", "parallel"), 48),
        name="hyena_out",
    )(y, u, x0, skip, w_bf, x, gt1, g2, sh2, sc2, wr)


def _gla_in_kernel(x_ref, g_ref, sh_ref, sc_ref, w_ref, wl_ref, wg_ref, bg_ref,
                   q_ref, k_ref, v_ref, r_ref, gf_ref, gb_ref, *, dk, dv, qscale):
    hn = _norm_mod(x_ref[0], g_ref[...], sc_ref[0], sh_ref[0]).astype(BF16)
    q_ref[0] = _bdot(hn, w_ref[:, 0:dk]) * qscale
    k_ref[0] = _bdot(hn, w_ref[:, dk:2 * dk])
    v_ref[0] = _bdot(hn, w_ref[:, 2 * dk:2 * dk + dv])
    r_ref[0] = _bdot(hn, w_ref[:, 2 * dk + dv:2 * dk + 2 * dv])
    low = _bdot(hn, wl_ref[...])
    gates = _log_sigmoid(_dot3(low, wg_ref[0], wg_ref[1]) + bg_ref[...]) * (1.0 / GLA_TAU)
    gf_ref[0] = gates[:, 0:dk]
    gb_ref[0] = gates[:, dk:2 * dk]


def _gla_in(x, g, sh, sc, w_main_bf, w_low_bf, w_gate, b_gate, dk, dv, tm=512):
    b, s, d = x.shape
    tm = min(tm, s)
    qscale = float((dk // GLA_HEADS) ** -0.5)
    kern = functools.partial(_gla_in_kernel, dk=dk, dv=dv, qscale=qscale)
    tokk = pl.BlockSpec((1, tm, dk), lambda i, j: (i, j, 0))
    tokv = pl.BlockSpec((1, tm, dv), lambda i, j: (i, j, 0))
    return pl.pallas_call(
        kern,
        out_shape=(jax.ShapeDtypeStruct((b, s, dk), F32),
                   jax.ShapeDtypeStruct((b, s, dk), F32),
                   jax.ShapeDtypeStruct((b, s, dv), F32),
                   jax.ShapeDtypeStruct((b, s, dv), F32),
                   jax.ShapeDtypeStruct((b, s, dk), F32),
                   jax.ShapeDtypeStruct((b, s, dk), F32)),
        grid=(b, s // tm),
        in_specs=[
            pl.BlockSpec((1, tm, d), lambda i, j: (i, j, 0)),
            pl.BlockSpec((1, d), lambda i, j: (0, 0)),
            pl.BlockSpec((1, 1, d), lambda i, j: (i, 0, 0)),
            pl.BlockSpec((1, 1, d), lambda i, j: (i, 0, 0)),
            pl.BlockSpec((d, 2 * dk + 2 * dv), lambda i, j: (0, 0)),
            pl.BlockSpec((d, LANES), lambda i, j: (0, 0)),
            pl.BlockSpec((2, LANES, 2 * dk), lambda i, j: (0, 0, 0)),
            pl.BlockSpec((1, 2 * dk), lambda i, j: (0, 0)),
        ],
        out_specs=(tokk, tokk, tokv, tokv, tokk, tokk),
        compiler_params=_params(("parallel", "parallel"), 56),
        name="gla_in",
    )(x, g, sh, sc, w_main_bf, w_low_bf, w_gate, b_gate)


def _gla_core_kernel(q_ref, k_ref, v_ref, gf_ref, gb_ref, o_ref, sf_ref, sb_ref,
                     *, ngroup):
    C = GLA_CHUNK
    G = GLA_GROUP
    R = C * G
    ri = lax.broadcasted_iota(I32, (C, C), 0)
    ci = lax.broadcasted_iota(I32, (C, C), 1)
    mask_f = ci <= ri
    mask_b = ci > ri
    rowc = lax.broadcasted_iota(I32, (R, 1), 0) % C
    sf_ref[...] = jnp.zeros_like(sf_ref)
    sb_ref[...] = jnp.zeros_like(sb_ref)
    nt = (((1,), (1,)), ((), ()))
    tn = (((0,), (0,)), ((), ()))

    def chunk_cumsum(x, reverse):
        sh = 1
        while sh < C:
            if reverse:
                x = x + jnp.where(rowc < C - sh, pltpu.roll(x, R - sh, 0), 0.0)
            else:
                x = x + jnp.where(rowc >= sh, pltpu.roll(x, sh, 0), 0.0)
            sh *= 2
        return x

    def group(gi, g_ref, s_ref, reverse, accumulate):
        rows = pl.ds(pl.multiple_of(gi * R, R), R)
        q = q_ref[0, rows, :]
        k = k_ref[0, rows, :]
        v = v_ref[0, rows, :].astype(BF16)
        cum = chunk_cumsum(g_ref[0, rows, :], reverse)
        q_dec = (q * jnp.exp(cum)).astype(BF16)
        k_inv = (k * jnp.exp(-cum)).astype(BF16)
        st = s_ref[...]
        outs = [None] * G
        for n in (range(G - 1, -1, -1) if reverse else range(G)):
            sl = slice(n * C, (n + 1) * C)
            edge = n * C if reverse else n * C + C - 1
            tot = cum[edge:edge + 1, :]
            k_end = (k[sl] * jnp.exp(tot - cum[sl])).astype(BF16)
            att = lax.dot_general(q_dec[sl], k_inv[sl], nt, preferred_element_type=F32)
            att = jnp.where(mask_b if reverse else mask_f, att, 0.0).astype(BF16)
            outs[n] = _bdot(att, v[sl]) + lax.dot_general(
                q_dec[sl], st.astype(BF16), nt, preferred_element_type=F32)
            st = st * jnp.exp(tot) + lax.dot_general(
                v[sl], k_end, tn, preferred_element_type=F32)
        s_ref[...] = st
        o = jnp.concatenate(outs, axis=0)
        if accumulate:
            o_ref[0, rows, :] += o
        else:
            o_ref[0, rows, :] = o

    def first_half(i, carry):
        group(i, gf_ref, sf_ref, False, False)
        group(ngroup - 1 - i, gb_ref, sb_ref, True, False)
        return carry

    def second_half(i, carry):
        group(i, gf_ref, sf_ref, False, True)
        group(ngroup - 1 - i, gb_ref, sb_ref, True, True)
        return carry

    half = ngroup // 2
    lax.fori_loop(0, half, first_half, 0)
    lax.fori_loop(half, ngroup, second_half, 0)


def _gla_core(q, k, v, gf, gb):
    b, s, dk = q.shape
    dv = v.shape[-1]
    hk = dk // GLA_HEADS
    hv = dv // GLA_HEADS
    ngroup = s // (GLA_CHUNK * GLA_GROUP)
    assert ngroup % 2 == 0
    kern = functools.partial(_gla_core_kernel, ngroup=ngroup)
    speck = pl.BlockSpec((1, s, hk), lambda i, h: (i, 0, h))
    specv = pl.BlockSpec((1, s, hv), lambda i, h: (i, 0, h))
    return pl.pallas_call(
        kern,
        out_shape=jax.ShapeDtypeStruct((b, s, dv), F32),
        grid=(b, GLA_HEADS),
        in_specs=[speck, speck, specv, speck, speck],
        out_specs=specv,
        scratch_shapes=[pltpu.VMEM((hv, hk), F32), pltpu.VMEM((hv, hk), F32)],
        compiler_params=_params(("parallel", "parallel"), 48),
        name="gla_core",
    )(q, k, v, gf, gb)


def _gla_out_kernel(o_ref, r_ref, ng_ref, w_ref, x_ref, gt_ref, g2_ref, sh2_ref,
                    sc2_ref, wr_ref, xo_ref, hn_ref, lg_ref, *, hv):
    o = o_ref[0]
    parts = []
    for h in range(GLA_HEADS):
        oh = o[:, h * hv:(h + 1) * hv]
        ms = jnp.mean(oh * oh, axis=-1, keepdims=True)
        parts.append(oh * lax.rsqrt(ms + EPS))
    on = jnp.concatenate(parts, axis=-1) * ng_ref[...]
    a = on * _silu(r_ref[0])
    _mixer_tail(a.astype(BF16), w_ref, x_ref, gt_ref, g2_ref, sh2_ref, sc2_ref,
                wr_ref, xo_ref, hn_ref, lg_ref)


def _gla_out(o, r, ng, w_bf, x, gt1, g2, sh2, sc2, wr, tm=512):
    b, s, d = x.shape
    dv = o.shape[-1]
    tm = min(tm, s)
    tok, vec, bvec, wsq, wrs, out_shape, out_specs = _tail_specs(b, s, d, tm)
    tokv = pl.BlockSpec((1, tm, dv), lambda i, j: (i, j, 0))
    kern = functools.partial(_gla_out_kernel, hv=dv // GLA_HEADS)
    return pl.pallas_call(
        kern,
        out_shape=out_shape,
        grid=(b, s // tm),
        in_specs=[tokv, tokv, pl.BlockSpec((1, dv), lambda i, j: (0, 0)),
                  pl.BlockSpec((dv, d), lambda i, j: (0, 0)),
                  tok, bvec, vec, bvec, bvec, wrs],
        out_specs=out_specs,
        compiler_params=_params(("parallel", "parallel"), 48),
        name="gla_out",
    )(o, r, ng, w_bf, x, gt1, g2, sh2, sc2, wr)


def _prefix_sum_lanes(x, n):
    lane = lax.broadcasted_iota(I32, x.shape, 1)
    sh = 1
    while sh < n:
        x = x + jnp.where(lane >= sh, pltpu.roll(x, sh, 1), 0)
        sh *= 2
    return x


def _route_kernel(lg_ref, slot_t_ref, slot_ref, aff_ref, toff_ref, *, cap, seq):
    E = N_EXPERTS
    lg = lg_ref[0]
    lane = lax.broadcasted_iota(I32, lg.shape, 1)
    live = lane < E
    lgm = jnp.where(live, lg, -jnp.inf)
    mx = jnp.max(lgm, axis=-1, keepdims=True)
    ex = jnp.where(live, jnp.exp(lgm - mx), 0.0)
    aff = ex / jnp.sum(ex, axis=-1, keepdims=True)
    aff_t = aff.T[0:E, :]
    aff_ref[0] = aff_t
    bits = pltpu.bitcast(aff_t, I32)

    def search(_, lohi):
        lo, hi = lohi
        mid = lo + (hi - lo + 1) // 2
        cnt = jnp.sum((bits >= mid).astype(I32), axis=1, keepdims=True)
        ok = cnt >= cap
        return jnp.where(ok, mid, lo), jnp.where(ok, hi, mid - 1)

    lo0 = jnp.zeros((E, 1), I32)
    hi0 = jnp.full((E, 1), 0x7F800000, I32)
    thr, _ = lax.fori_loop(0, 32, search, (lo0, hi0))
    gt = bits > thr
    eq = bits == thr
    need = cap - jnp.sum(gt.astype(I32), axis=1, keepdims=True)
    eq_i = eq.astype(I32)
    eq_rank = _prefix_sum_lanes(eq_i, seq) - eq_i
    sel = jnp.logical_or(gt, jnp.logical_and(eq, eq_rank < need))
    sel_i = sel.astype(I32)
    pos = _prefix_sum_lanes(sel_i, seq) - sel_i
    slot_t = jnp.where(sel, pos, -1)
    slot_t_ref[0] = slot_t
    pad = jnp.full((LANES - E, seq), -1, I32)
    slot_ref[0] = jnp.concatenate([slot_t, pad], axis=0).T
    lane_e = lax.broadcasted_iota(I32, (E, LANES), 1)
    run = jnp.zeros((E, 1), I32)
    toff = jnp.zeros((E, LANES), I32)
    for j in range(seq // MOE_TILE):
        run = run + jnp.sum(sel_i[:, j * MOE_TILE:(j + 1) * MOE_TILE], axis=1, keepdims=True)
        toff = toff + jnp.where(lane_e == j + 1, run, 0)
    toff_ref[0] = toff


def _route(logits, cap):
    b, s, _ = logits.shape
    assert s // MOE_TILE < LANES
    kern = functools.partial(_route_kernel, cap=cap, seq=s)
    return pl.pallas_call(
        kern,
        out_shape=(jax.ShapeDtypeStruct((b, N_EXPERTS, s), I32),
                   jax.ShapeDtypeStruct((b, s, LANES), I32),
                   jax.ShapeDtypeStruct((b, N_EXPERTS, s), F32),
                   jax.ShapeDtypeStruct((b, N_EXPERTS, LANES), I32)),
        grid=(b,),
        in_specs=[pl.BlockSpec((1, s, LANES), lambda i: (i, 0, 0))],
        out_specs=(pl.BlockSpec((1, N_EXPERTS, s), lambda i: (i, 0, 0)),
                   pl.BlockSpec((1, s, LANES), lambda i: (i, 0, 0)),
                   pl.BlockSpec((1, N_EXPERTS, s), lambda i: (i, 0, 0)),
                   pl.BlockSpec((1, N_EXPERTS, LANES), lambda i: (i, 0, 0))),
        compiler_params=_params(("parallel",), 48),
        name="moe_route",
    )(logits)


def _tile_range(toff_ref, base, nt, lo, hi):
    jlo = jnp.int32(0)
    jend = jnp.int32(0)
    for j in range(nt):
        jlo = jlo + jnp.where(toff_ref[base + j + 1] <= lo, 1, 0)
        jend = jend + jnp.where(toff_ref[base + j] < hi, 1, 0)
    return jlo, jend


def _moe_ffn_kernel(toff_ref, hn_ref, slot_ref, aff_ref, wg_ref, wu_ref, wd_ref,
                    y_ref, xg_ref, acc_ref, gacc_ref, gate_ref, *, cap, seq):
    f = pl.program_id(2)
    nf = pl.num_programs(2)

    @pl.when(f == 0)
    def _():
        base = (pl.program_id(0) * pl.num_programs(1) + pl.program_id(1)) * LANES
        nt = seq // MOE_TILE
        wt = min(MOE_WIN_TILES, nt)
        r = lax.broadcasted_iota(I32, (MOE_SLOT_BLOCK, 1), 0)

        def gathered(t0, width, r0):
            hit = slot_ref[0, 0, :, pl.ds(t0, width)] == r + r0
            onehot = jnp.where(hit, 1.0, 0.0).astype(BF16)
            gate = jnp.sum(jnp.where(hit, aff_ref[0, 0, :, pl.ds(t0, width)], 0.0),
                           axis=1, keepdims=True)
            return _bdot(onehot, hn_ref[0, pl.ds(t0, width), :]), gate

        for kb in range(cap // MOE_SLOT_BLOCK):
            r0 = kb * MOE_SLOT_BLOCK
            rows = slice(r0, r0 + MOE_SLOT_BLOCK)
            jlo, jend = _tile_range(toff_ref, base, nt, r0, r0 + MOE_SLOT_BLOCK)
            jw = jnp.minimum(jlo, nt - wt)
            xw, gw = gathered(pl.multiple_of(jw * MOE_TILE, MOE_TILE), wt * MOE_TILE, r0)
            gacc_ref[...] = xw
            gate_ref[rows, :] = gw

            def tile(j, carry, rows=rows, r0=r0):
                xj, gj = gathered(pl.multiple_of(j * MOE_TILE, MOE_TILE), MOE_TILE, r0)
                gacc_ref[...] += xj
                gate_ref[rows, :] += gj
                return carry

            lax.fori_loop(jnp.maximum(jlo, jw + wt), jend, tile, 0)
            xg_ref[rows, :] = gacc_ref[...].astype(BF16)
        acc_ref[...] = jnp.zeros_like(acc_ref)

    xg = xg_ref[...]
    a = _bdot(xg, wg_ref[0])
    u = _bdot(xg, wu_ref[0])
    hh = (_silu(a) * u).astype(BF16)
    acc_ref[...] += _bdot(hh, wd_ref[0])

    @pl.when(f == nf - 1)
    def _():
        y_ref[0, 0] = (acc_ref[...] * gate_ref[...]).astype(BF16)


def _moe_ffn(hn_bf, slot_t, aff_t, toff, wg_bf, wu_bf, wd_bf, cap, fc=1024):
    b, s, d = hn_bf.shape
    e, _, fdim = wg_bf.shape
    fc = min(fc, fdim)
    assert cap % MOE_SLOT_BLOCK == 0 and s % MOE_TILE == 0
    kern = functools.partial(_moe_ffn_kernel, cap=cap, seq=s)
    grid_spec = pltpu.PrefetchScalarGridSpec(
        num_scalar_prefetch=1,
        grid=(b, e, fdim // fc),
        in_specs=[
            pl.BlockSpec((1, s, d), lambda i, j, f, t: (i, 0, 0)),
            pl.BlockSpec((1, 1, 1, s), lambda i, j, f, t: (i, j, 0, 0)),
            pl.BlockSpec((1, 1, 1, s), lambda i, j, f, t: (i, j, 0, 0)),
            pl.BlockSpec((1, d, fc), lambda i, j, f, t: (j, 0, f)),
            pl.BlockSpec((1, d, fc), lambda i, j, f, t: (j, 0, f)),
            pl.BlockSpec((1, fc, d), lambda i, j, f, t: (j, f, 0)),
        ],
        out_specs=pl.BlockSpec((1, 1, cap, d), lambda i, j, f, t: (i, j, 0, 0)),
        scratch_shapes=[pltpu.VMEM((cap, d), BF16), pltpu.VMEM((cap, d), F32),
                        pltpu.VMEM((MOE_SLOT_BLOCK, d), F32), pltpu.VMEM((cap, 1), F32)],
    )
    return pl.pallas_call(
        kern,
        out_shape=jax.ShapeDtypeStruct((b, e, cap, d), BF16),
        grid_spec=grid_spec,
        compiler_params=_params(("parallel", "arbitrary", "arbitrary"), 56),
        name="moe_ffn",
    )(toff.reshape(-1), hn_bf, slot_t.reshape(b, e, 1, s), aff_t.reshape(b, e, 1, s),
      wg_bf, wu_bf, wd_bf)


def _moe_combine_kernel(toff_ref, x_ref, gt_ref, slot_ref, y_ref, gfin_ref, o_ref,
                        ycat_ref, oh_ref, acc_ref, *, cap, final, tt, ne):
    nsub = tt // MOE_TILE
    kw = min(MOE_TILE, cap)
    slot = slot_ref[0]
    unfit = []
    for e in range(ne):
        base = (pl.program_id(0) * ne + e) * LANES + pl.program_id(1) * nsub
        lo = toff_ref[base]
        hi = toff_ref[base + nsub]
        start = jnp.minimum((lo // BF16_ROWS) * BF16_ROWS, cap - kw)
        s0 = pl.multiple_of(start, BF16_ROWS)
        fits = hi <= start + kw
        unfit.append(jnp.logical_not(fits))
        ycat_ref[e * kw:(e + 1) * kw, :] = y_ref[0, e, pl.ds(s0, kw), :]
        r = lax.broadcasted_iota(I32, (1, kw), 1) + s0
        hit = jnp.logical_and(slot[:, e:e + 1] == r, fits)
        oh_ref[:, e * kw:(e + 1) * kw] = jnp.where(hit, 1.0, 0.0).astype(BF16)
    acc_ref[...] = _bdot(oh_ref[...], ycat_ref[...])

    for e in range(ne):

        @pl.when(unfit[e])
        def _(e=e):
            r = lax.broadcasted_iota(I32, (1, cap), 1)
            onehot = jnp.where(slot[:, e:e + 1] == r, 1.0, 0.0).astype(BF16)
            acc_ref[...] += _bdot(onehot, y_ref[0, e])

    xn = x_ref[0] + gt_ref[0] * acc_ref[...]
    if final:
        ms = jnp.mean(xn * xn, axis=-1, keepdims=True)
        xn = xn * lax.rsqrt(ms + EPS) * gfin_ref[...]
    o_ref[0] = xn


def _moe_combine(x, gt2, slot, toff, y, gfin, cap, final, tt=512):
    b, s, d = x.shape
    e = y.shape[1]
    tt = min(tt, s)
    kw = min(MOE_TILE, cap)
    assert tt % MOE_TILE == 0 and cap % kw == 0
    kern = functools.partial(_moe_combine_kernel, cap=cap, final=final, tt=tt, ne=e)
    grid_spec = pltpu.PrefetchScalarGridSpec(
        num_scalar_prefetch=1,
        grid=(b, s // tt),
        in_specs=[
            pl.BlockSpec((1, tt, d), lambda i, t, o: (i, t, 0)),
            pl.BlockSpec((1, 1, d), lambda i, t, o: (i, 0, 0)),
            pl.BlockSpec((1, tt, LANES), lambda i, t, o: (i, t, 0)),
            pl.BlockSpec((1, e, cap, d), lambda i, t, o: (i, 0, 0, 0),
                         pipeline_mode=pl.Buffered(1)),
            pl.BlockSpec((1, d), lambda i, t, o: (0, 0)),
        ],
        out_specs=pl.BlockSpec((1, tt, d), lambda i, t, o: (i, t, 0)),
        scratch_shapes=[pltpu.VMEM((e * kw, d), BF16), pltpu.VMEM((tt, e * kw), BF16),
                        pltpu.VMEM((tt, d), F32)],
    )
    return pl.pallas_call(
        kern,
        out_shape=jax.ShapeDtypeStruct((b, s, d), F32),
        grid_spec=grid_spec,
        compiler_params=_params(("parallel", "arbitrary"), 56),
        name="moe_combine",
    )(toff.reshape(-1), x, gt2, slot, y, gfin)


def _hi_lo(w):
    hi = w.astype(BF16)
    return jnp.stack([hi, (w - hi.astype(F32)).astype(BF16)])


def kernel(x, c, w_ada, b_ada, norm_mix, norm_ffn, hy_w_in, hy_conv_w, hy_conv_b, hy_f_w1, hy_f_b1, hy_f_w2, hy_f_b2, hy_f_w3, hy_f_freq, hy_skip, hy_w_out, gla_w_in, gla_w_gf, gla_b_gf, gla_w_gb, gla_b_gb, gla_norm, gla_w_out, moe_router, moe_w_gate, moe_w_up, moe_w_down, norm_final):
    b, s, d = x.shape
    depth = w_ada.shape[0]
    n_exp = moe_router.shape[-1]
    cap = EC_CAPACITY * s // n_exp
    dk = gla_w_gf.shape[-1]
    dv = gla_w_out.shape[1]
    nb = s // CONV_BLOCK

    mod = _adaln(c, w_ada, b_ada)
    for i in range(depth):
        j = i // 2
        sh1, sc1, gt1, sh2, sc2, gt2 = (mod[i, m].reshape(b, 1, d) for m in range(N_MOD))
        g1 = norm_mix[i][None, :]
        g2 = norm_ffn[i][None, :]
        wr = _hi_lo(jnp.zeros((d, LANES), F32).at[:, :n_exp].set(moe_router[i]))
        if i % 2 == 0:
            x0, u = _hy_in(x, g1, sh1, sc1, hy_w_in[j].astype(BF16), hy_conv_w[j],
                           hy_conv_b[j][None, :])
            gt, nrm = _hy_filter(s, hy_f_w1[j], hy_f_b1[j], hy_f_w2[j], hy_f_b2[j],
                                 hy_f_w3[j], hy_f_freq[j])
            ut = u.reshape(b, nb, CONV_BLOCK, d).transpose(3, 1, 0, 2)
            yt = _hy_conv(gt, nrm, ut.reshape(d, nb * b, CONV_BLOCK), nb, b)
            y = yt.reshape(d, nb, b, CONV_BLOCK).transpose(2, 1, 3, 0).reshape(b, s, d)
            x, hn, lg = _hy_out(y, u, x0, hy_skip[j][None, :], hy_w_out[j].astype(BF16),
                                x, gt1, g2, sh2, sc2, wr)
        else:
            w_in = gla_w_in[j]
            w_low = jnp.zeros((d, LANES), F32).at[:, :2 * GLA_GATE_RANK].set(
                w_in[:, 2 * dk + 2 * dv:]).astype(BF16)
            w_gate = jnp.zeros((LANES, 2 * dk), F32)
            w_gate = w_gate.at[:GLA_GATE_RANK, :dk].set(gla_w_gf[j])
            w_gate = w_gate.at[GLA_GATE_RANK:2 * GLA_GATE_RANK, dk:].set(gla_w_gb[j])
            b_gate = jnp.concatenate([gla_b_gf[j], gla_b_gb[j]])[None, :]
            q, k, v, r, gf, gb = _gla_in(x, g1, sh1, sc1,
                                         w_in[:, :2 * dk + 2 * dv].astype(BF16),
                                         w_low, _hi_lo(w_gate), b_gate, dk, dv)
            o = _gla_core(q, k, v, gf, gb)
            x, hn, lg = _gla_out(o, r, gla_norm[j].reshape(1, dv),
                                 gla_w_out[j].astype(BF16), x, gt1, g2, sh2, sc2, wr)
        slot_t, slot, aff_t, toff = _route(lg, cap)
        y = _moe_ffn(hn, slot_t, aff_t, toff, moe_w_gate[i].astype(BF16),
                     moe_w_up[i].astype(BF16), moe_w_down[i].astype(BF16), cap)
        x = _moe_combine(x, gt2, slot, toff, y, norm_final[None, :], cap,
                         final=(i == depth - 1))
    return x
```

```python
import functools
import math

import jax
import jax.numpy as jnp
from jax import lax
from jax.experimental import pallas as pl
from jax.experimental.pallas import tpu as pltpu

F32 = jnp.float32
BF16 = jnp.bfloat16
I32 = jnp.int32
HIGHEST = lax.Precision.HIGHEST

EPS = 1e-6
N_MOD = 6
LANES = 128
VMEM_LIMIT_CAP = 56 << 20

SHORT_CONV = 3
FILTER_BANDS = 16
DECAY_TARGET = 1e-2
FAST_DECAY_PCT = 0.3
SLOW_DECAY_PCT = 1.5
MIN_DECAY = math.log(DECAY_TARGET) / SLOW_DECAY_PCT
MAX_DECAY = math.log(DECAY_TARGET) / FAST_DECAY_PCT
CONV_BLOCK = 256
CONV_CH = 2


GLA_HEADS = 4
GLA_GATE_RANK = 16
GLA_TAU = 16.0
GLA_CHUNK = 64
GLA_GROUP = 4

N_EXPERTS = 16
EC_CAPACITY = 2
MOE_TILE = 256
MOE_SLOT_BLOCK = 128
MOE_WIN_TILES = 6
BF16_ROWS = 16


def _params(sem, vmem_mb):
    return pltpu.CompilerParams(
        dimension_semantics=sem,
        vmem_limit_bytes=min(vmem_mb << 20, VMEM_LIMIT_CAP))


def _sigmoid(x):
    return 1.0 / (1.0 + jnp.exp(-x))


def _silu(x):
    return x * _sigmoid(x)


def _log_sigmoid(x):
    return jnp.minimum(x, 0.0) - jnp.log(1.0 + jnp.exp(-jnp.abs(x)))


def _norm_mod(x, g, sc, sh):
    ms = jnp.mean(x * x, axis=-1, keepdims=True)
    return (x * lax.rsqrt(ms + EPS) * g) * (1.0 + sc) + sh


def _bdot(a, b):
    return jnp.dot(a, b, preferred_element_type=F32)


def _hdot(a, b):
    return jnp.dot(a, b, preferred_element_type=F32, precision=HIGHEST)


def _split_bf16(a):
    hi = a.astype(BF16)
    return hi, (a - hi.astype(F32)).astype(BF16)


def _dot3(a, b_hi, b_lo):
    a_hi, a_lo = _split_bf16(a)
    return _bdot(a_hi, b_hi) + _bdot(a_hi, b_lo) + _bdot(a_lo, b_hi)


def _adaln_kernel(c_ref, w_ref, b_ref, o_ref):
    cond = _silu(c_ref[...])
    o_ref[0, 0] = _hdot(cond, w_ref[0]) + b_ref[0]


def _adaln(c, w_ada, b_ada):
    depth, d, _ = w_ada.shape
    b = c.shape[0]
    return pl.pallas_call(
        _adaln_kernel,
        out_shape=jax.ShapeDtypeStruct((depth, N_MOD, b, d), F32),
        grid=(depth, N_MOD),
        in_specs=[
            pl.BlockSpec((b, d), lambda i, n: (0, 0)),
            pl.BlockSpec((1, d, d), lambda i, n: (i, 0, n)),
            pl.BlockSpec((1, 1, d), lambda i, n: (i, 0, n)),
        ],
        out_specs=pl.BlockSpec((1, 1, b, d), lambda i, n: (i, n, 0, 0)),
        compiler_params=_params(("parallel", "parallel"), 32),
        name="adaln_mod",
    )(c, w_ada, b_ada.reshape(depth, 1, N_MOD * d))


def _hy_in_kernel(x_ref, xp_ref, xn_ref, g_ref, sh_ref, sc_ref, w_ref, cw_ref,
                  cb_ref, x0_ref, u_ref, *, tm, d):
    s = pl.program_id(1)
    ns = pl.num_programs(1)
    g = g_ref[...]
    sh = sh_ref[0]
    sc = sc_ref[0]
    hn = _norm_mod(x_ref[0], g, sc, sh).astype(BF16)
    hp = _norm_mod(xp_ref[0], g, sc, sh).astype(BF16)
    hx = _norm_mod(xn_ref[0], g, sc, sh).astype(BF16)
    row = lax.broadcasted_iota(I32, (tm, 1), 0)
    outs = []
    for part in range(3):
        cols = slice(part * d, (part + 1) * d)
        w = w_ref[:, cols]
        p = _bdot(hn, w)
        pp = jnp.where(s > 0, _bdot(hp, w)[7:8, :], 0.0)
        pn = jnp.where(s < ns - 1, _bdot(hx, w)[0:1, :], 0.0)
        p_prev = jnp.where(row == 0, pp, pltpu.roll(p, 1, 0))
        p_next = jnp.where(row == tm - 1, pn, pltpu.roll(p, tm - 1, 0))
        cw = cw_ref[:, cols]
        outs.append(cw[0:1] * p_prev + cw[1:2] * p + cw[2:3] * p_next
                    + cb_ref[:, cols])
    x0_ref[0] = outs[0]
    u_ref[0] = outs[2] * outs[1]


def _hy_in(x, g, sh, sc, w_bf, conv_w, conv_b, tm=512):
    b, s, d = x.shape
    tm = min(tm, s)
    hb = tm // 8
    nhb = s // 8
    kern = functools.partial(_hy_in_kernel, tm=tm, d=d)
    return pl.pallas_call(
        kern,
        out_shape=(jax.ShapeDtypeStruct((b, s, d), F32),
                   jax.ShapeDtypeStruct((b, s, d), F32)),
        grid=(b, s // tm),
        in_specs=[
            pl.BlockSpec((1, tm, d), lambda i, j: (i, j, 0)),
            pl.BlockSpec((1, 8, d), lambda i, j: (i, jnp.maximum(j * hb - 1, 0), 0)),
            pl.BlockSpec((1, 8, d), lambda i, j: (i, jnp.minimum((j + 1) * hb, nhb - 1), 0)),
            pl.BlockSpec((1, d), lambda i, j: (0, 0)),
            pl.BlockSpec((1, 1, d), lambda i, j: (i, 0, 0)),
            pl.BlockSpec((1, 1, d), lambda i, j: (i, 0, 0)),
            pl.BlockSpec((d, 3 * d), lambda i, j: (0, 0)),
            pl.BlockSpec((SHORT_CONV, 3 * d), lambda i, j: (0, 0)),
            pl.BlockSpec((1, 3 * d), lambda i, j: (0, 0)),
        ],
        out_specs=(pl.BlockSpec((1, tm, d), lambda i, j: (i, j, 0)),
                   pl.BlockSpec((1, tm, d), lambda i, j: (i, j, 0))),
        compiler_params=_params(("parallel", "parallel"), 56),
        name="hyena_in",
    )(x, x, x, g, sh, sc, w_bf, conv_w, conv_b)


FILT_T_COL = 127
FILT_VALID_COL = 126


def _hy_filter_kernel(z_ref, w1_ref, b1_ref, w2_ref, b2_ref, w3_ref, fr_ref,
                      dl_ref, gt_ref, nrm_ref):
    xb = pl.program_id(0)
    z = z_ref[...]
    fr = fr_ref[...]
    h = jnp.sin(fr * (_hdot(z, w1_ref[...]) + b1_ref[...]))
    h = jnp.sin(fr * (_hdot(h, w2_ref[...]) + b2_ref[...]))
    h = _hdot(h, w3_ref[0])
    t = z[:, FILT_T_COL:FILT_T_COL + 1]
    valid = z[:, FILT_VALID_COL:FILT_VALID_COL + 1]
    h = h * jnp.exp(-t * dl_ref[...]) * valid
    ht = h.T
    gt_ref[...] = ht
    part = jnp.sum(jnp.abs(ht), axis=1, keepdims=True)

    @pl.when(xb == 0)
    def _():
        nrm_ref[...] = jnp.zeros_like(nrm_ref)

    nrm_ref[...] += jnp.broadcast_to(part, nrm_ref.shape)


def _hy_filter(seq, w1, b1, w2, b2, w3, freq, xt=512):
    L = seq
    d = w3.shape[-1] // 2
    hid = w2.shape[0]
    emb = w1.shape[0]
    xt = min(xt, L)
    t = jnp.linspace(0.0, 1.0, L, dtype=F32)
    bands = jnp.linspace(1e-4, FILTER_BANDS - 1, FILTER_BANDS, dtype=F32)
    ang = (2.0 * math.pi / L) * jnp.arange(L, dtype=F32)[:, None] * bands[None, :]
    zf = jnp.concatenate([t[:, None], jnp.cos(ang), -jnp.sin(ang)], axis=-1)
    lag = jnp.abs(jnp.arange(2 * L) - L)
    valid = (lag < L).astype(F32)
    lagc = jnp.minimum(lag, L - 1)
    z2 = jnp.zeros((2 * L, LANES), F32)
    z2 = z2.at[:, :emb].set(zf[lagc])
    z2 = z2.at[:, FILT_T_COL].set(t[lagc])
    z2 = z2.at[:, FILT_VALID_COL].set(valid)
    w1p = jnp.zeros((LANES, hid), F32).at[:emb].set(w1)
    w3d = w3.reshape(hid, 2, d).transpose(1, 0, 2)
    deltas = jnp.abs(jnp.linspace(MIN_DECAY, MAX_DECAY, d, dtype=F32))[None, :]
    nxb = 2 * L // xt
    half = nxb // 2
    return pl.pallas_call(
        _hy_filter_kernel,
        out_shape=(jax.ShapeDtypeStruct((d, 2 * L), F32),
                   jax.ShapeDtypeStruct((d, LANES), F32)),
        grid=(nxb,),
        in_specs=[
            pl.BlockSpec((xt, LANES), lambda i: (i, 0)),
            pl.BlockSpec((LANES, hid), lambda i: (0, 0)),
            pl.BlockSpec((1, hid), lambda i: (0, 0)),
            pl.BlockSpec((hid, hid), lambda i: (0, 0)),
            pl.BlockSpec((1, hid), lambda i: (0, 0)),
            pl.BlockSpec((1, hid, d), lambda i: (jnp.where(i >= half, 0, 1), 0, 0)),
            pl.BlockSpec((1, hid), lambda i: (0, 0)),
            pl.BlockSpec((1, d), lambda i: (0, 0)),
        ],
        out_specs=(pl.BlockSpec((d, xt), lambda i: (0, i)),
                   pl.BlockSpec((d, LANES), lambda i: (0, 0))),
        compiler_params=_params(("arbitrary",), 32),
        name="hyena_filter",
    )(z2, w1p, b1[None, :], w2, b2[None, :], w3d, freq[None, :], deltas)


def _hy_conv_kernel(g_ref, nrm_ref, u_ref, y_ref, pg_ref, strip_ref, ucat_ref,
                    *, nb, bsz):
    P = CONV_BLOCK
    nshift = 2 * nb - 1
    nwin = 2 * nb * P // LANES
    ngrp = nshift * P // LANES
    zoff = nshift * P // LANES
    HR = LANES // 2
    hb = nb // 2
    nsh = nb + hb - 1
    hrows = hb * bsz

    @pl.when(pl.program_id(0) == 0)
    def _():
        ucat_ref[...] = jnp.zeros_like(ucat_ref)

    row = lax.broadcasted_iota(I32, (HR, LANES), 0)
    col = lax.broadcasted_iota(I32, (HR, LANES), 1)
    upper = col >= 2 * row
    nq = P // LANES

    for c in range(CONV_CH):
        g = g_ref[c]
        gb = pltpu.bitcast(g.astype(BF16).astype(F32), I32)
        gm1 = pltpu.bitcast(pltpu.roll(g, 1, 1).astype(BF16).astype(F32), I32)
        pg_ref[c] = jnp.bitwise_or(jnp.bitwise_and(jnp.right_shift(gb, 16), 0xFFFF),
                                   jnp.bitwise_and(gm1, -65536))

        def rolled(w, c=c):
            pw = pg_ref[c, :, w * LANES:(w + 1) * LANES]
            return pltpu.roll(jnp.broadcast_to(pw, (HR, LANES)), 0, 1, stride=2,
                              stride_axis=0)

        cur = rolled(nwin - 1)
        for w in range(nwin - 1, 0, -1):
            prev = rolled(w - 1)
            blk = pltpu.bitcast(jnp.where(upper, cur, prev), BF16)
            for q in range(nq):
                m = zoff + q - w
                if 0 <= m < ngrp:
                    strip_ref[c, m * LANES:(m + 1) * LANES, q * LANES:(q + 1) * LANES] = blk
            cur = prev

        u = u_ref[c].astype(BF16)
        for m in range(nsh):
            il0 = max(0, hb - 1 - m)
            il1 = min(hb, nb + hb - 1 - m)
            j0 = il0 + m - (hb - 1)
            ucat_ref[c, il0 * bsz:il1 * bsz, m * P:(m + 1) * P] = (
                u[j0 * bsz:(j0 + il1 - il0) * bsz])
        inv = 1.0 / nrm_ref[c, :, 0:1]
        for h in range(2):
            off = (nb - 1 - (h * hb + hb - 1)) * P
            y = _bdot(ucat_ref[c], strip_ref[c, off:off + nsh * P, :])
            y_ref[c, h * hrows:(h + 1) * hrows, :] = y * inv


def _hy_conv(gt, nrm, ut, nb, bsz):
    d = ut.shape[0]
    P = CONV_BLOCK
    rows = nb * bsz
    nshift = 2 * nb - 1
    ch = CONV_CH
    assert nb % 2 == 0 and d % ch == 0
    kern = functools.partial(_hy_conv_kernel, nb=nb, bsz=bsz)
    return pl.pallas_call(
        kern,
        out_shape=jax.ShapeDtypeStruct((d, rows, P), F32),
        grid=(d // ch,),
        in_specs=[
            pl.BlockSpec((ch, 1, 2 * nb * P), lambda i: (i, 0, 0)),
            pl.BlockSpec((ch, 1, LANES), lambda i: (i, 0, 0)),
            pl.BlockSpec((ch, rows, P), lambda i: (i, 0, 0)),
        ],
        out_specs=pl.BlockSpec((ch, rows, P), lambda i: (i, 0, 0)),
        scratch_shapes=[
            pltpu.VMEM((ch, 1, 2 * nb * P), I32),
            pltpu.VMEM((ch, nshift * P, P), BF16),
            pltpu.VMEM((ch, rows // 2, (nb + nb // 2 - 1) * P), BF16),
        ],
        compiler_params=_params(("arbitrary",), 48),
        name="hyena_conv",
    )(gt.reshape(d, 1, 2 * nb * P), nrm.reshape(d, 1, LANES), ut)


def _mixer_tail(a_bf, w_ref, x_ref, gt_ref, g2_ref, sh2_ref, sc2_ref, wr_ref,
                xo_ref, hn_ref, lg_ref):
    xn = x_ref[0] + gt_ref[0] * _bdot(a_bf, w_ref[...])
    xo_ref[0] = xn
    hn = _norm_mod(xn, g2_ref[...], sc2_ref[0], sh2_ref[0])
    hn_ref[0] = hn.astype(BF16)
    lg_ref[0] = _dot3(hn, wr_ref[0], wr_ref[1])


def _hy_out_kernel(y_ref, u_ref, x0_ref, skip_ref, w_ref, x_ref, gt_ref, g2_ref,
                   sh2_ref, sc2_ref, wr_ref, xo_ref, hn_ref, lg_ref):
    a = (y_ref[0] + u_ref[0] * skip_ref[...]) * x0_ref[0]
    _mixer_tail(a.astype(BF16), w_ref, x_ref, gt_ref, g2_ref, sh2_ref, sc2_ref,
                wr_ref, xo_ref, hn_ref, lg_ref)


def _tail_specs(b, s, d, tm):
    tok = pl.BlockSpec((1, tm, d), lambda i, j: (i, j, 0))
    vec = pl.BlockSpec((1, d), lambda i, j: (0, 0))
    bvec = pl.BlockSpec((1, 1, d), lambda i, j: (i, 0, 0))
    wsq = pl.BlockSpec((d, d), lambda i, j: (0, 0))
    wr = pl.BlockSpec((2, d, LANES), lambda i, j: (0, 0, 0))
    out_shape = (jax.ShapeDtypeStruct((b, s, d), F32),
                 jax.ShapeDtypeStruct((b, s, d), BF16),
                 jax.ShapeDtypeStruct((b, s, LANES), F32))
    out_specs = (tok, tok, pl.BlockSpec((1, tm, LANES), lambda i, j: (i, j, 0)))
    return tok, vec, bvec, wsq, wr, out_shape, out_specs


def _hy_out(y, u, x0, skip, w_bf, x, gt1, g2, sh2, sc2, wr, tm=512):
    b, s, d = x.shape
    tm = min(tm, s)
    tok, vec, bvec, wsq, wrs, out_shape, out_specs = _tail_specs(b, s, d, tm)
    return pl.pallas_call(
        _hy_out_kernel,
        out_shape=out_shape,
        grid=(b, s // tm),
        in_specs=[tok, tok, tok, vec, wsq, tok, bvec, vec, bvec, bvec, wrs],
        out_specs=out_specs,
        compiler_params=_params(("parallel", "parallel"), 48),
        name="hyena_out",
    )(y, u, x0, skip, w_bf, x, gt1, g2, sh2, sc2, wr)


def _gla_in_kernel(x_ref, g_ref, sh_ref, sc_ref, w_ref, wl_ref, wg_ref, bg_ref,
                   q_ref, k_ref, v_ref, r_ref, gf_ref, gb_ref, *, dk, dv, qscale):
    hn = _norm_mod(x_ref[0], g_ref[...], sc_ref[0], sh_ref[0]).astype(BF16)
    q_ref[0] = _bdot(hn, w_ref[:, 0:dk]) * qscale
    k_ref[0] = _bdot(hn, w_ref[:, dk:2 * dk])
    v_ref[0] = _bdot(hn, w_ref[:, 2 * dk:2 * dk + dv])
    r_ref[0] = _bdot(hn, w_ref[:, 2 * dk + dv:2 * dk + 2 * dv])
    low = _bdot(hn, wl_ref[...])
    gates = _log_sigmoid(_dot3(low, wg_ref[0], wg_ref[1]) + bg_ref[...]) * (1.0 / GLA_TAU)
    gf_ref[0] = gates[:, 0:dk]
    gb_ref[0] = gates[:, dk:2 * dk]


def _gla_in(x, g, sh, sc, w_main_bf, w_low_bf, w_gate, b_gate, dk, dv, tm=512):
    b, s, d = x.shape
    tm = min(tm, s)
    qscale = float((dk // GLA_HEADS) ** -0.5)
    kern = functools.partial(_gla_in_kernel, dk=dk, dv=dv, qscale=qscale)
    tokk = pl.BlockSpec((1, tm, dk), lambda i, j: (i, j, 0))
    tokv = pl.BlockSpec((1, tm, dv), lambda i, j: (i, j, 0))
    return pl.pallas_call(
        kern,
        out_shape=(jax.ShapeDtypeStruct((b, s, dk), F32),
                   jax.ShapeDtypeStruct((b, s, dk), F32),
                   jax.ShapeDtypeStruct((b, s, dv), F32),
                   jax.ShapeDtypeStruct((b, s, dv), F32),
                   jax.ShapeDtypeStruct((b, s, dk), F32),
                   jax.ShapeDtypeStruct((b, s, dk), F32)),
        grid=(b, s // tm),
        in_specs=[
            pl.BlockSpec((1, tm, d), lambda i, j: (i, j, 0)),
            pl.BlockSpec((1, d), lambda i, j: (0, 0)),
            pl.BlockSpec((1, 1, d), lambda i, j: (i, 0, 0)),
            pl.BlockSpec((1, 1, d), lambda i, j: (i, 0, 0)),
            pl.BlockSpec((d, 2 * dk + 2 * dv), lambda i, j: (0, 0)),
            pl.BlockSpec((d, LANES), lambda i, j: (0, 0)),
            pl.BlockSpec((2, LANES, 2 * dk), lambda i, j: (0, 0, 0)),
            pl.BlockSpec((1, 2 * dk), lambda i, j: (0, 0)),
        ],
        out_specs=(tokk, tokk, tokv, tokv, tokk, tokk),
        compiler_params=_params(("parallel", "parallel"), 56),
        name="gla_in",
    )(x, g, sh, sc, w_main_bf, w_low_bf, w_gate, b_gate)


def _gla_core_kernel(q_ref, k_ref, v_ref, gf_ref, gb_ref, o_ref, sf_ref, sb_ref,
                     *, ngroup):
    C = GLA_CHUNK
    G = GLA_GROUP
    R = C * G
    ri = lax.broadcasted_iota(I32, (C, C), 0)
    ci = lax.broadcasted_iota(I32, (C, C), 1)
    mask_f = ci <= ri
    mask_b = ci > ri
    rowc = lax.broadcasted_iota(I32, (R, 1), 0) % C
    sf_ref[...] = jnp.zeros_like(sf_ref)
    sb_ref[...] = jnp.zeros_like(sb_ref)
    nt = (((1,), (1,)), ((), ()))
    tn = (((0,), (0,)), ((), ()))

    def chunk_cumsum(x, reverse):
        sh = 1
        while sh < C:
            if reverse:
                x = x + jnp.where(rowc < C - sh, pltpu.roll(x, R - sh, 0), 0.0)
            else:
                x = x + jnp.where(rowc >= sh, pltpu.roll(x, sh, 0), 0.0)
            sh *= 2
        return x

    def group(gi, g_ref, s_ref, reverse, accumulate):
        rows = pl.ds(pl.multiple_of(gi * R, R), R)
        q = q_ref[0, rows, :]
        k = k_ref[0, rows, :]
        v = v_ref[0, rows, :].astype(BF16)
        cum = chunk_cumsum(g_ref[0, rows, :], reverse)
        q_dec = (q * jnp.exp(cum)).astype(BF16)
        k_inv = (k * jnp.exp(-cum)).astype(BF16)
        st = s_ref[...]
        outs = [None] * G
        for n in (range(G - 1, -1, -1) if reverse else range(G)):
            sl = slice(n * C, (n + 1) * C)
            edge = n * C if reverse else n * C + C - 1
            tot = cum[edge:edge + 1, :]
            k_end = (k[sl] * jnp.exp(tot - cum[sl])).astype(BF16)
            att = lax.dot_general(q_dec[sl], k_inv[sl], nt, preferred_element_type=F32)
            att = jnp.where(mask_b if reverse else mask_f, att, 0.0).astype(BF16)
            outs[n] = _bdot(att, v[sl]) + lax.dot_general(
                q_dec[sl], st.astype(BF16), nt, preferred_element_type=F32)
            st = st * jnp.exp(tot) + lax.dot_general(
                v[sl], k_end, tn, preferred_element_type=F32)
        s_ref[...] = st
        o = jnp.concatenate(outs, axis=0)
        if accumulate:
            o_ref[0, rows, :] += o
        else:
            o_ref[0, rows, :] = o

    def first_half(i, carry):
        group(i, gf_ref, sf_ref, False, False)
        group(ngroup - 1 - i, gb_ref, sb_ref, True, False)
        return carry

    def second_half(i, carry):
        group(i, gf_ref, sf_ref, False, True)
        group(ngroup - 1 - i, gb_ref, sb_ref, True, True)
        return carry

    half = ngroup // 2
    lax.fori_loop(0, half, first_half, 0)
    lax.fori_loop(half, ngroup, second_half, 0)


def _gla_core(q, k, v, gf, gb):
    b, s, dk = q.shape
    dv = v.shape[-1]
    hk = dk // GLA_HEADS
    hv = dv // GLA_HEADS
    ngroup = s // (GLA_CHUNK * GLA_GROUP)
    assert ngroup % 2 == 0
    kern = functools.partial(_gla_core_kernel, ngroup=ngroup)
    speck = pl.BlockSpec((1, s, hk), lambda i, h: (i, 0, h))
    specv = pl.BlockSpec((1, s, hv), lambda i, h: (i, 0, h))
    return pl.pallas_call(
        kern,
        out_shape=jax.ShapeDtypeStruct((b, s, dv), F32),
        grid=(b, GLA_HEADS),
        in_specs=[speck, speck, specv, speck, speck],
        out_specs=specv,
        scratch_shapes=[pltpu.VMEM((hv, hk), F32), pltpu.VMEM((hv, hk), F32)],
        compiler_params=_params(("parallel", "parallel"), 48),
        name="gla_core",
    )(q, k, v, gf, gb)


def _gla_out_kernel(o_ref, r_ref, ng_ref, w_ref, x_ref, gt_ref, g2_ref, sh2_ref,
                    sc2_ref, wr_ref, xo_ref, hn_ref, lg_ref, *, hv):
    o = o_ref[0]
    parts = []
    for h in range(GLA_HEADS):
        oh = o[:, h * hv:(h + 1) * hv]
        ms = jnp.mean(oh * oh, axis=-1, keepdims=True)
        parts.append(oh * lax.rsqrt(ms + EPS))
    on = jnp.concatenate(parts, axis=-1) * ng_ref[...]
    a = on * _silu(r_ref[0])
    _mixer_tail(a.astype(BF16), w_ref, x_ref, gt_ref, g2_ref, sh2_ref, sc2_ref,
                wr_ref, xo_ref, hn_ref, lg_ref)


def _gla_out(o, r, ng, w_bf, x, gt1, g2, sh2, sc2, wr, tm=512):
    b, s, d = x.shape
    dv = o.shape[-1]
    tm = min(tm, s)
    tok, vec, bvec, wsq, wrs, out_shape, out_specs = _tail_specs(b, s, d, tm)
    tokv = pl.BlockSpec((1, tm, dv), lambda i, j: (i, j, 0))
    kern = functools.partial(_gla_out_kernel, hv=dv // GLA_HEADS)
    return pl.pallas_call(
        kern,
        out_shape=out_shape,
        grid=(b, s // tm),
        in_specs=[tokv, tokv, pl.BlockSpec((1, dv), lambda i, j: (0, 0)),
                  pl.BlockSpec((dv, d), lambda i, j: (0, 0)),
                  tok, bvec, vec, bvec, bvec, wrs],
        out_specs=out_specs,
        compiler_params=_params(("parallel", "parallel"), 48),
        name="gla_out",
    )(o, r, ng, w_bf, x, gt1, g2, sh2, sc2, wr)


def _prefix_sum_lanes(x, n):
    lane = lax.broadcasted_iota(I32, x.shape, 1)
    sh = 1
    while sh < n:
        x = x + jnp.where(lane >= sh, pltpu.roll(x, sh, 1), 0)
        sh *= 2
    return x


def _route_kernel(lg_ref, slot_t_ref, slot_ref, aff_ref, toff_ref, *, cap, seq):
    E = N_EXPERTS
    lg = lg_ref[0]
    lane = lax.broadcasted_iota(I32, lg.shape, 1)
    live = lane < E
    lgm = jnp.where(live, lg, -jnp.inf)
    mx = jnp.max(lgm, axis=-1, keepdims=True)
    ex = jnp.where(live, jnp.exp(lgm - mx), 0.0)
    aff = ex / jnp.sum(ex, axis=-1, keepdims=True)
    aff_t = aff.T[0:E, :]
    aff_ref[0] = aff_t
    bits = pltpu.bitcast(aff_t, I32)

    def search(_, lohi):
        lo, hi = lohi
        mid = lo + (hi - lo + 1) // 2
        cnt = jnp.sum((bits >= mid).astype(I32), axis=1, keepdims=True)
        ok = cnt >= cap
        return jnp.where(ok, mid, lo), jnp.where(ok, hi, mid - 1)

    lo0 = jnp.zeros((E, 1), I32)
    hi0 = jnp.full((E, 1), 0x7F800000, I32)
    thr, _ = lax.fori_loop(0, 32, search, (lo0, hi0))
    gt = bits > thr
    eq = bits == thr
    need = cap - jnp.sum(gt.astype(I32), axis=1, keepdims=True)
    eq_i = eq.astype(I32)
    eq_rank = _prefix_sum_lanes(eq_i, seq) - eq_i
    sel = jnp.logical_or(gt, jnp.logical_and(eq, eq_rank < need))
    sel_i = sel.astype(I32)
    pos = _prefix_sum_lanes(sel_i, seq) - sel_i
    slot_t = jnp.where(sel, pos, -1)
    slot_t_ref[0] = slot_t
    pad = jnp.full((LANES - E, seq), -1, I32)
    slot_ref[0] = jnp.concatenate([slot_t, pad], axis=0).T
    lane_e = lax.broadcasted_iota(I32, (E, LANES), 1)
    run = jnp.zeros((E, 1), I32)
    toff = jnp.zeros((E, LANES), I32)
    for j in range(seq // MOE_TILE):
        run = run + jnp.sum(sel_i[:, j * MOE_TILE:(j + 1) * MOE_TILE], axis=1, keepdims=True)
        toff = toff + jnp.where(lane_e == j + 1, run, 0)
    toff_ref[0] = toff


def _route(logits, cap):
    b, s, _ = logits.shape
    assert s // MOE_TILE < LANES
    kern = functools.partial(_route_kernel, cap=cap, seq=s)
    return pl.pallas_call(
        kern,
        out_shape=(jax.ShapeDtypeStruct((b, N_EXPERTS, s), I32),
                   jax.ShapeDtypeStruct((b, s, LANES), I32),
                   jax.ShapeDtypeStruct((b, N_EXPERTS, s), F32),
                   jax.ShapeDtypeStruct((b, N_EXPERTS, LANES), I32)),
        grid=(b,),
        in_specs=[pl.BlockSpec((1, s, LANES), lambda i: (i, 0, 0))],
        out_specs=(pl.BlockSpec((1, N_EXPERTS, s), lambda i: (i, 0, 0)),
                   pl.BlockSpec((1, s, LANES), lambda i: (i, 0, 0)),
                   pl.BlockSpec((1, N_EXPERTS, s), lambda i: (i, 0, 0)),
                   pl.BlockSpec((1, N_EXPERTS, LANES), lambda i: (i, 0, 0))),
        compiler_params=_params(("parallel",), 48),
        name="moe_route",
    )(logits)


def _tile_range(toff_ref, base, nt, lo, hi):
    jlo = jnp.int32(0)
    jend = jnp.int32(0)
    for j in range(nt):
        jlo = jlo + jnp.where(toff_ref[base + j + 1] <= lo, 1, 0)
        jend = jend + jnp.where(toff_ref[base + j] < hi, 1, 0)
    return jlo, jend


def _moe_ffn_kernel(toff_ref, hn_ref, slot_ref, aff_ref, wg_ref, wu_ref, wd_ref,
                    y_ref, xg_ref, acc_ref, gacc_ref, gate_ref, *, cap, seq):
    f = pl.program_id(2)
    nf = pl.num_programs(2)

    @pl.when(f == 0)
    def _():
        base = (pl.program_id(0) * pl.num_programs(1) + pl.program_id(1)) * LANES
        nt = seq // MOE_TILE
        wt = min(MOE_WIN_TILES, nt)
        r = lax.broadcasted_iota(I32, (MOE_SLOT_BLOCK, 1), 0)

        def gathered(t0, width, r0):
            hit = slot_ref[0, 0, :, pl.ds(t0, width)] == r + r0
            onehot = jnp.where(hit, 1.0, 0.0).astype(BF16)
            gate = jnp.sum(jnp.where(hit, aff_ref[0, 0, :, pl.ds(t0, width)], 0.0),
                           axis=1, keepdims=True)
            return _bdot(onehot, hn_ref[0, pl.ds(t0, width), :]), gate

        for kb in range(cap // MOE_SLOT_BLOCK):
            r0 = kb * MOE_SLOT_BLOCK
            rows = slice(r0, r0 + MOE_SLOT_BLOCK)
            jlo, jend = _tile_range(toff_ref, base, nt, r0, r0 + MOE_SLOT_BLOCK)
            jw = jnp.minimum(jlo, nt - wt)
            xw, gw = gathered(pl.multiple_of(jw * MOE_TILE, MOE_TILE), wt * MOE_TILE, r0)
            gacc_ref[...] = xw
            gate_ref[rows, :] = gw

            def tile(j, carry, rows=rows, r0=r0):
                xj, gj = gathered(pl.multiple_of(j * MOE_TILE, MOE_TILE), MOE_TILE, r0)
                gacc_ref[...] += xj
                gate_ref[rows, :] += gj
                return carry

            lax.fori_loop(jnp.maximum(jlo, jw + wt), jend, tile, 0)
            xg_ref[rows, :] = gacc_ref[...].astype(BF16)
        acc_ref[...] = jnp.zeros_like(acc_ref)

    xg = xg_ref[...]
    a = _bdot(xg, wg_ref[0])
    u = _bdot(xg, wu_ref[0])
    hh = (_silu(a) * u).astype(BF16)
    acc_ref[...] += _bdot(hh, wd_ref[0])

    @pl.when(f == nf - 1)
    def _():
        y_ref[0, 0] = (acc_ref[...] * gate_ref[...]).astype(BF16)


def _moe_ffn(hn_bf, slot_t, aff_t, toff, wg_bf, wu_bf, wd_bf, cap, fc=1024):
    b, s, d = hn_bf.shape
    e, _, fdim = wg_bf.shape
    fc = min(fc, fdim)
    assert cap % MOE_SLOT_BLOCK == 0 and s % MOE_TILE == 0
    kern = functools.partial(_moe_ffn_kernel, cap=cap, seq=s)
    grid_spec = pltpu.PrefetchScalarGridSpec(
        num_scalar_prefetch=1,
        grid=(b, e, fdim // fc),
        in_specs=[
            pl.BlockSpec((1, s, d), lambda i, j, f, t: (i, 0, 0)),
            pl.BlockSpec((1, 1, 1, s), lambda i, j, f, t: (i, j, 0, 0)),
            pl.BlockSpec((1, 1, 1, s), lambda i, j, f, t: (i, j, 0, 0)),
            pl.BlockSpec((1, d, fc), lambda i, j, f, t: (j, 0, f)),
            pl.BlockSpec((1, d, fc), lambda i, j, f, t: (j, 0, f)),
            pl.BlockSpec((1, fc, d), lambda i, j, f, t: (j, f, 0)),
        ],
        out_specs=pl.BlockSpec((1, 1, cap, d), lambda i, j, f, t: (i, j, 0, 0)),
        scratch_shapes=[pltpu.VMEM((cap, d), BF16), pltpu.VMEM((cap, d), F32),
                        pltpu.VMEM((MOE_SLOT_BLOCK, d), F32), pltpu.VMEM((cap, 1), F32)],
    )
    return pl.pallas_call(
        kern,
        out_shape=jax.ShapeDtypeStruct((b, e, cap, d), BF16),
        grid_spec=grid_spec,
        compiler_params=_params(("parallel", "arbitrary", "arbitrary"), 56),
        name="moe_ffn",
    )(toff.reshape(-1), hn_bf, slot_t.reshape(b, e, 1, s), aff_t.reshape(b, e, 1, s),
      wg_bf, wu_bf, wd_bf)


def _moe_combine_kernel(toff_ref, x_ref, gt_ref, slot_ref, y_ref, gfin_ref, o_ref,
                        ycat_ref, oh_ref, acc_ref, *, cap, final, tt, ne):
    nsub = tt // MOE_TILE
    kw = min(MOE_TILE, cap)
    slot = slot_ref[0]
    unfit = []
    for e in range(ne):
        base = (pl.program_id(0) * ne + e) * LANES + pl.program_id(1) * nsub
        lo = toff_ref[base]
        hi = toff_ref[base + nsub]
        start = jnp.minimum((lo // BF16_ROWS) * BF16_ROWS, cap - kw)
        s0 = pl.multiple_of(start, BF16_ROWS)
        fits = hi <= start + kw
        unfit.append(jnp.logical_not(fits))
        ycat_ref[e * kw:(e + 1) * kw, :] = y_ref[0, e, pl.ds(s0, kw), :]
        r = lax.broadcasted_iota(I32, (1, kw), 1) + s0
        hit = jnp.logical_and(slot[:, e:e + 1] == r, fits)
        oh_ref[:, e * kw:(e + 1) * kw] = jnp.where(hit, 1.0, 0.0).astype(BF16)
    acc_ref[...] = _bdot(oh_ref[...], ycat_ref[...])

    for e in range(ne):

        @pl.when(unfit[e])
        def _(e=e):
            r = lax.broadcasted_iota(I32, (1, cap), 1)
            onehot = jnp.where(slot[:, e:e + 1] == r, 1.0, 0.0).astype(BF16)
            acc_ref[...] += _bdot(onehot, y_ref[0, e])

    xn = x_ref[0] + gt_ref[0] * acc_ref[...]
    if final:
        ms = jnp.mean(xn * xn, axis=-1, keepdims=True)
        xn = xn * lax.rsqrt(ms + EPS) * gfin_ref[...]
    o_ref[0] = xn


def _moe_combine(x, gt2, slot, toff, y, gfin, cap, final, tt=512):
    b, s, d = x.shape
    e = y.shape[1]
    tt = min(tt, s)
    kw = min(MOE_TILE, cap)
    assert tt % MOE_TILE == 0 and cap % kw == 0
    kern = functools.partial(_moe_combine_kernel, cap=cap, final=final, tt=tt, ne=e)
    grid_spec = pltpu.PrefetchScalarGridSpec(
        num_scalar_prefetch=1,
        grid=(b, s // tt),
        in_specs=[
            pl.BlockSpec((1, tt, d), lambda i, t, o: (i, t, 0)),
            pl.BlockSpec((1, 1, d), lambda i, t, o: (i, 0, 0)),
            pl.BlockSpec((1, tt, LANES), lambda i, t, o: (i, t, 0)),
            pl.BlockSpec((1, e, cap, d), lambda i, t, o: (i, 0, 0, 0),
                         pipeline_mode=pl.Buffered(1)),
            pl.BlockSpec((1, d), lambda i, t, o: (0, 0)),
        ],
        out_specs=pl.BlockSpec((1, tt, d), lambda i, t, o: (i, t, 0)),
        scratch_shapes=[pltpu.VMEM((e * kw, d), BF16), pltpu.VMEM((tt, e * kw), BF16),
                        pltpu.VMEM((tt, d), F32)],
    )
    return pl.pallas_call(
        kern,
        out_shape=jax.ShapeDtypeStruct((b, s, d), F32),
        grid_spec=grid_spec,
        compiler_params=_params(("parallel", "arbitrary"), 56),
        name="moe_combine",
    )(toff.reshape(-1), x, gt2, slot, y, gfin)


def _hi_lo(w):
    hi = w.astype(BF16)
    return jnp.stack([hi, (w - hi.astype(F32)).astype(BF16)])


def kernel(x, c, w_ada, b_ada, norm_mix, norm_ffn, hy_w_in, hy_conv_w, hy_conv_b, hy_f_w1, hy_f_b1, hy_f_w2, hy_f_b2, hy_f_w3, hy_f_freq, hy_skip, hy_w_out, gla_w_in, gla_w_gf, gla_b_gf, gla_w_gb, gla_b_gb, gla_norm, gla_w_out, moe_router, moe_w_gate, moe_w_up, moe_w_down, norm_final):
    b, s, d = x.shape
    depth = w_ada.shape[0]
    n_exp = moe_router.shape[-1]
    cap = EC_CAPACITY * s // n_exp
    dk = gla_w_gf.shape[-1]
    dv = gla_w_out.shape[1]
    nb = s // CONV_BLOCK

    mod = _adaln(c, w_ada, b_ada)
    for i in range(depth):
        j = i // 2
        sh1, sc1, gt1, sh2, sc2, gt2 = (mod[i, m].reshape(b, 1, d) for m in range(N_MOD))
        g1 = norm_mix[i][None, :]
        g2 = norm_ffn[i][None, :]
        wr = _hi_lo(jnp.zeros((d, LANES), F32).at[:, :n_exp].set(moe_router[i]))
        if i % 2 == 0:
            x0, u = _hy_in(x, g1, sh1, sc1, hy_w_in[j].astype(BF16), hy_conv_w[j],
                           hy_conv_b[j][None, :])
            gt, nrm = _hy_filter(s, hy_f_w1[j], hy_f_b1[j], hy_f_w2[j], hy_f_b2[j],
                                 hy_f_w3[j], hy_f_freq[j])
            ut = u.reshape(b, nb, CONV_BLOCK, d).transpose(3, 1, 0, 2)
            yt = _hy_conv(gt, nrm, ut.reshape(d, nb * b, CONV_BLOCK), nb, b)
            y = yt.reshape(d, nb, b, CONV_BLOCK).transpose(2, 1, 3, 0).reshape(b, s, d)
            x, hn, lg = _hy_out(y, u, x0, hy_skip[j][None, :], hy_w_out[j].astype(BF16),
                                x, gt1, g2, sh2, sc2, wr)
        else:
            w_in = gla_w_in[j]
            w_low = jnp.zeros((d, LANES), F32).at[:, :2 * GLA_GATE_RANK].set(
                w_in[:, 2 * dk + 2 * dv:]).astype(BF16)
            w_gate = jnp.zeros((LANES, 2 * dk), F32)
            w_gate = w_gate.at[:GLA_GATE_RANK, :dk].set(gla_w_gf[j])
            w_gate = w_gate.at[GLA_GATE_RANK:2 * GLA_GATE_RANK, dk:].set(gla_w_gb[j])
            b_gate = jnp.concatenate([gla_b_gf[j], gla_b_gb[j]])[None, :]
            q, k, v, r, gf, gb = _gla_in(x, g1, sh1, sc1,
                                         w_in[:, :2 * dk + 2 * dv].astype(BF16),
                                         w_low, _hi_lo(w_gate), b_gate, dk, dv)
            o = _gla_core(q, k, v, gf, gb)
            x, hn, lg = _gla_out(o, r, gla_norm[j].reshape(1, dv),
                                 gla_w_out[j].astype(BF16), x, gt1, g2, sh2, sc2, wr)
        slot_t, slot, aff_t, toff = _route(lg, cap)
        y = _moe_ffn(hn, slot_t, aff_t, toff, moe_w_gate[i].astype(BF16),
                     moe_w_up[i].astype(BF16), moe_w_down[i].astype(BF16), cap)
        x = _moe_combine(x, gt2, slot, toff, y, norm_final[None, :], cap,
                         final=(i == depth - 1))
    return x
```

```python
import functools
import math

import jax
import jax.numpy as jnp
import numpy as np
from jax import lax
from jax.experimental import pallas as pl
from jax.experimental.pallas import tpu as pltpu

F32 = jnp.float32
BF16 = jnp.bfloat16
I32 = jnp.int32
HIGHEST = lax.Precision.HIGHEST

EPS = 1e-6
N_MOD = 6
LANES = 128
VMEM_LIMIT_CAP = 56 << 20

SHORT_CONV = 3
FILTER_BANDS = 16
DECAY_TARGET = 1e-2
FAST_DECAY_PCT = 0.3
SLOW_DECAY_PCT = 1.5
MIN_DECAY = math.log(DECAY_TARGET) / SLOW_DECAY_PCT
MAX_DECAY = math.log(DECAY_TARGET) / FAST_DECAY_PCT
CONV_BLOCK = 256
CONV_CH = 4


GLA_HEADS = 4
GLA_GATE_RANK = 16
GLA_TAU = 16.0
GLA_CHUNK = 64
GLA_GROUP = 8

N_EXPERTS = 16
EC_CAPACITY = 2
MOE_TILE = 256
MOE_SLOT_BLOCK = 128
MOE_WIN_TILES = 6
BF16_ROWS = 16


def _params(sem, vmem_mb):
    return pltpu.CompilerParams(
        dimension_semantics=sem,
        vmem_limit_bytes=min(vmem_mb << 20, VMEM_LIMIT_CAP))


def _sigmoid(x):
    return 1.0 / (1.0 + jnp.exp(-x))


def _silu(x):
    return x * _sigmoid(x)


def _log_sigmoid(x):
    return jnp.minimum(x, 0.0) - jnp.log(1.0 + jnp.exp(-jnp.abs(x)))


def _norm_mod(x, g, sc, sh):
    ms = jnp.mean(x * x, axis=-1, keepdims=True)
    return (x * lax.rsqrt(ms + EPS) * g) * (1.0 + sc) + sh


def _bdot(a, b):
    return jnp.dot(a, b, preferred_element_type=F32)


def _hdot(a, b):
    return jnp.dot(a, b, preferred_element_type=F32, precision=HIGHEST)


def _split_bf16(a):
    hi = a.astype(BF16)
    return hi, (a - hi.astype(F32)).astype(BF16)


def _dot3(a, b_hi, b_lo):
    a_hi, a_lo = _split_bf16(a)
    return _bdot(a_hi, b_hi) + _bdot(a_hi, b_lo) + _bdot(a_lo, b_hi)


def _adaln_kernel(c_ref, w_ref, b_ref, o_ref):
    cond = _silu(c_ref[...])
    o_ref[0, 0] = _hdot(cond, w_ref[0]) + b_ref[0]


def _adaln(c, w_ada, b_ada):
    depth, d, _ = w_ada.shape
    b = c.shape[0]
    return pl.pallas_call(
        _adaln_kernel,
        out_shape=jax.ShapeDtypeStruct((depth, N_MOD, b, d), F32),
        grid=(depth, N_MOD),
        in_specs=[
            pl.BlockSpec((b, d), lambda i, n: (0, 0)),
            pl.BlockSpec((1, d, d), lambda i, n: (i, 0, n)),
            pl.BlockSpec((1, 1, d), lambda i, n: (i, 0, n)),
        ],
        out_specs=pl.BlockSpec((1, 1, b, d), lambda i, n: (i, n, 0, 0)),
        compiler_params=_params(("parallel", "parallel"), 32),
        name="adaln_mod",
    )(c, w_ada, b_ada.reshape(depth, 1, N_MOD * d))


def _hy_in_kernel(x_ref, xp_ref, xn_ref, g_ref, sh_ref, sc_ref, w_ref, cw_ref,
                  cb_ref, x0_ref, u_ref, *, tm, d):
    s = pl.program_id(1)
    ns = pl.num_programs(1)
    g = g_ref[...]
    sh = sh_ref[0]
    sc = sc_ref[0]
    hn = _norm_mod(x_ref[0], g, sc, sh).astype(BF16)
    hp = _norm_mod(xp_ref[0], g, sc, sh).astype(BF16)
    hx = _norm_mod(xn_ref[0], g, sc, sh).astype(BF16)
    row = lax.broadcasted_iota(I32, (tm, 1), 0)
    outs = []
    for part in range(3):
        cols = slice(part * d, (part + 1) * d)
        w = w_ref[:, cols]
        p = _bdot(hn, w)
        pp = jnp.where(s > 0, _bdot(hp, w)[7:8, :], 0.0)
        pn = jnp.where(s < ns - 1, _bdot(hx, w)[0:1, :], 0.0)
        p_prev = jnp.where(row == 0, pp, pltpu.roll(p, 1, 0))
        p_next = jnp.where(row == tm - 1, pn, pltpu.roll(p, tm - 1, 0))
        cw = cw_ref[:, cols]
        outs.append(cw[0:1] * p_prev + cw[1:2] * p + cw[2:3] * p_next
                    + cb_ref[:, cols])
    x0_ref[0] = outs[0]
    u_ref[0] = outs[2] * outs[1]


def _hy_in(x, g, sh, sc, w_bf, conv_w, conv_b, tm=512):
    b, s, d = x.shape
    tm = min(tm, s)
    hb = tm // 8
    nhb = s // 8
    kern = functools.partial(_hy_in_kernel, tm=tm, d=d)
    return pl.pallas_call(
        kern,
        out_shape=(jax.ShapeDtypeStruct((b, s, d), F32),
                   jax.ShapeDtypeStruct((b, s, d), F32)),
        grid=(b, s // tm),
        in_specs=[
            pl.BlockSpec((1, tm, d), lambda i, j: (i, j, 0)),
            pl.BlockSpec((1, 8, d), lambda i, j: (i, jnp.maximum(j * hb - 1, 0), 0)),
            pl.BlockSpec((1, 8, d), lambda i, j: (i, jnp.minimum((j + 1) * hb, nhb - 1), 0)),
            pl.BlockSpec((1, d), lambda i, j: (0, 0)),
            pl.BlockSpec((1, 1, d), lambda i, j: (i, 0, 0)),
            pl.BlockSpec((1, 1, d), lambda i, j: (i, 0, 0)),
            pl.BlockSpec((d, 3 * d), lambda i, j: (0, 0)),
            pl.BlockSpec((SHORT_CONV, 3 * d), lambda i, j: (0, 0)),
            pl.BlockSpec((1, 3 * d), lambda i, j: (0, 0)),
        ],
        out_specs=(pl.BlockSpec((1, tm, d), lambda i, j: (i, j, 0)),
                   pl.BlockSpec((1, tm, d), lambda i, j: (i, j, 0))),
        compiler_params=_params(("parallel", "parallel"), 56),
        name="hyena_in",
    )(x, x, x, g, sh, sc, w_bf, conv_w, conv_b)


FILT_T_COL = 127
FILT_VALID_COL = 126


def _hy_filter_kernel(z_ref, w1_ref, b1_ref, w2_ref, b2_ref, w3_ref, fr_ref,
                      dl_ref, gt_ref, nrm_ref):
    xb = pl.program_id(0)
    z = z_ref[...]
    fr = fr_ref[...]
    h = jnp.sin(fr * (_hdot(z, w1_ref[...]) + b1_ref[...]))
    h = jnp.sin(fr * (_hdot(h, w2_ref[...]) + b2_ref[...]))
    h = _hdot(h, w3_ref[0])
    t = z[:, FILT_T_COL:FILT_T_COL + 1]
    valid = z[:, FILT_VALID_COL:FILT_VALID_COL + 1]
    h = h * jnp.exp(-t * dl_ref[...]) * valid
    ht = h.T
    gt_ref[...] = ht
    part = jnp.sum(jnp.abs(ht), axis=1, keepdims=True)

    @pl.when(xb == 0)
    def _():
        nrm_ref[...] = jnp.zeros_like(nrm_ref)

    nrm_ref[...] += jnp.broadcast_to(part, nrm_ref.shape)


@functools.lru_cache(maxsize=None)
def _filter_positions(L, emb):
    f32 = np.float32
    t = np.linspace(0.0, 1.0, L, dtype=f32)
    bands = np.linspace(1e-4, FILTER_BANDS - 1, FILTER_BANDS, dtype=f32)
    ang = f32(2.0 * math.pi / L) * np.arange(L, dtype=f32)[:, None] * bands[None, :]
    zf = np.concatenate([t[:, None], np.cos(ang), -np.sin(ang)], axis=-1).astype(f32)
    assert zf.shape[1] == emb
    lag = np.abs(np.arange(2 * L) - L)
    lagc = np.minimum(lag, L - 1)
    z2 = np.zeros((2 * L, LANES), f32)
    z2[:, :emb] = zf[lagc]
    z2[:, FILT_T_COL] = t[lagc]
    z2[:, FILT_VALID_COL] = (lag < L).astype(f32)
    return z2


def _hy_filter(seq, w1, b1, w2, b2, w3, freq, xt=512):
    L = seq
    d = w3.shape[-1] // 2
    hid = w2.shape[0]
    emb = w1.shape[0]
    xt = min(xt, L)
    z2 = jnp.asarray(_filter_positions(L, emb))
    w1p = jnp.zeros((LANES, hid), F32).at[:emb].set(w1)
    w3d = w3.reshape(hid, 2, d).transpose(1, 0, 2)
    deltas = jnp.abs(jnp.linspace(MIN_DECAY, MAX_DECAY, d, dtype=F32))[None, :]
    nxb = 2 * L // xt
    half = nxb // 2
    return pl.pallas_call(
        _hy_filter_kernel,
        out_shape=(jax.ShapeDtypeStruct((d, 2 * L), F32),
                   jax.ShapeDtypeStruct((d, LANES), F32)),
        grid=(nxb,),
        in_specs=[
            pl.BlockSpec((xt, LANES), lambda i: (i, 0)),
            pl.BlockSpec((LANES, hid), lambda i: (0, 0)),
            pl.BlockSpec((1, hid), lambda i: (0, 0)),
            pl.BlockSpec((hid, hid), lambda i: (0, 0)),
            pl.BlockSpec((1, hid), lambda i: (0, 0)),
            pl.BlockSpec((1, hid, d), lambda i: (jnp.where(i >= half, 0, 1), 0, 0)),
            pl.BlockSpec((1, hid), lambda i: (0, 0)),
            pl.BlockSpec((1, d), lambda i: (0, 0)),
        ],
        out_specs=(pl.BlockSpec((d, xt), lambda i: (0, i)),
                   pl.BlockSpec((d, LANES), lambda i: (0, 0))),
        compiler_params=_params(("arbitrary",), 32),
        name="hyena_filter",
    )(z2, w1p, b1[None, :], w2, b2[None, :], w3d, freq[None, :], deltas)


def _hy_conv_kernel(g_ref, nrm_ref, u_ref, y_ref, pg_ref, strip_ref, ucat_ref,
                    *, nb, bsz):
    P = CONV_BLOCK
    nshift = 2 * nb - 1
    nwin = 2 * nb * P // LANES
    ngrp = nshift * P // LANES
    zoff = nshift * P // LANES
    HR = LANES // 2
    hb = nb // 2
    nsh = nb + hb - 1
    hrows = hb * bsz

    @pl.when(pl.program_id(0) == 0)
    def _():
        ucat_ref[...] = jnp.zeros_like(ucat_ref)

    row = lax.broadcasted_iota(I32, (HR, LANES), 0)
    col = lax.broadcasted_iota(I32, (HR, LANES), 1)
    upper = col >= 2 * row
    nq = P // LANES

    for c in range(CONV_CH):
        g = g_ref[c]
        gb = pltpu.bitcast(g.astype(BF16).astype(F32), I32)
        gm1 = pltpu.bitcast(pltpu.roll(g, 1, 1).astype(BF16).astype(F32), I32)
        pg_ref[c] = jnp.bitwise_or(jnp.bitwise_and(jnp.right_shift(gb, 16), 0xFFFF),
                                   jnp.bitwise_and(gm1, -65536))

        def rolled(w, c=c):
            pw = pg_ref[c, :, w * LANES:(w + 1) * LANES]
            return pltpu.roll(jnp.broadcast_to(pw, (HR, LANES)), 0, 1, stride=2,
                              stride_axis=0)

        cur = rolled(nwin - 1)
        for w in range(nwin - 1, 0, -1):
            prev = rolled(w - 1)
            blk = pltpu.bitcast(jnp.where(upper, cur, prev), BF16)
            for q in range(nq):
                m = zoff + q - w
                if 0 <= m < ngrp:
                    strip_ref[c, m * LANES:(m + 1) * LANES, q * LANES:(q + 1) * LANES] = blk
            cur = prev

        u = u_ref[c].astype(BF16)
        for m in range(nsh):
            il0 = max(0, hb - 1 - m)
            il1 = min(hb, nb + hb - 1 - m)
            j0 = il0 + m - (hb - 1)
            ucat_ref[c, il0 * bsz:il1 * bsz, m * P:(m + 1) * P] = (
                u[j0 * bsz:(j0 + il1 - il0) * bsz])
        inv = 1.0 / nrm_ref[c, :, 0:1]
        for h in range(2):
            off = (nb - 1 - (h * hb + hb - 1)) * P
            y = _bdot(ucat_ref[c], strip_ref[c, off:off + nsh * P, :])
            y_ref[c, h * hrows:(h + 1) * hrows, :] = y * inv


def _hy_conv(gt, nrm, ut, nb, bsz):
    d = ut.shape[0]
    P = CONV_BLOCK
    rows = nb * bsz
    nshift = 2 * nb - 1
    ch = CONV_CH
    assert nb % 2 == 0 and d % ch == 0
    kern = functools.partial(_hy_conv_kernel, nb=nb, bsz=bsz)
    return pl.pallas_call(
        kern,
        out_shape=jax.ShapeDtypeStruct((d, rows, P), F32),
        grid=(d // ch,),
        in_specs=[
            pl.BlockSpec((ch, 1, 2 * nb * P), lambda i: (i, 0, 0)),
            pl.BlockSpec((ch, 1, LANES), lambda i: (i, 0, 0)),
            pl.BlockSpec((ch, rows, P), lambda i: (i, 0, 0)),
        ],
        out_specs=pl.BlockSpec((ch, rows, P), lambda i: (i, 0, 0)),
        scratch_shapes=[
            pltpu.VMEM((ch, 1, 2 * nb * P), I32),
            pltpu.VMEM((ch, nshift * P, P), BF16),
            pltpu.VMEM((ch, rows // 2, (nb + nb // 2 - 1) * P), BF16),
        ],
        compiler_params=_params(("arbitrary",), 48),
        name="hyena_conv",
    )(gt.reshape(d, 1, 2 * nb * P), nrm.reshape(d, 1, LANES), ut)


def _mixer_tail(a_bf, w_ref, x_ref, gt_ref, g2_ref, sh2_ref, sc2_ref, wr_ref,
                xo_ref, hn_ref, lg_ref):
    xn = x_ref[0] + gt_ref[0] * _bdot(a_bf, w_ref[...])
    xo_ref[0] = xn
    hn = _norm_mod(xn, g2_ref[...], sc2_ref[0], sh2_ref[0])
    hn_ref[0] = hn.astype(BF16)
    lg_ref[0] = _dot3(hn, wr_ref[0], wr_ref[1])


def _hy_out_kernel(y_ref, u_ref, x0_ref, skip_ref, w_ref, x_ref, gt_ref, g2_ref,
                   sh2_ref, sc2_ref, wr_ref, xo_ref, hn_ref, lg_ref):
    a = (y_ref[0] + u_ref[0] * skip_ref[...]) * x0_ref[0]
    _mixer_tail(a.astype(BF16), w_ref, x_ref, gt_ref, g2_ref, sh2_ref, sc2_ref,
                wr_ref, xo_ref, hn_ref, lg_ref)


def _tail_specs(b, s, d, tm):
    tok = pl.BlockSpec((1, tm, d), lambda i, j: (i, j, 0))
    vec = pl.BlockSpec((1, d), lambda i, j: (0, 0))
    bvec = pl.BlockSpec((1, 1, d), lambda i, j: (i, 0, 0))
    wsq = pl.BlockSpec((d, d), lambda i, j: (0, 0))
    wr = pl.BlockSpec((2, d, LANES), lambda i, j: (0, 0, 0))
    out_shape = (jax.ShapeDtypeStruct((b, s, d), F32),
                 jax.ShapeDtypeStruct((b, s, d), BF16),
                 jax.ShapeDtypeStruct((b, s, LANES), F32))
    out_specs = (tok, tok, pl.BlockSpec((1, tm, LANES), lambda i, j: (i, j, 0)))
    return tok, vec, bvec, wsq, wr, out_shape, out_specs


def _hy_out(y, u, x0, skip, w_bf, x, gt1, g2, sh2, sc2, wr, tm=512):
    b, s, d = x.shape
    tm = min(tm, s)
    tok, vec, bvec, wsq, wrs, out_shape, out_specs = _tail_specs(b, s, d, tm)
    return pl.pallas_call(
        _hy_out_kernel,
        out_shape=out_shape,
        grid=(b, s // tm),
        in_specs=[tok, tok, tok, vec, wsq, tok, bvec, vec, bvec, bvec, wrs],
        out_specs=out_specs,
        compiler_params=_params(("parallel", "parallel"), 48),
        name="hyena_out",
    )(y, u, x0, skip, w_bf, x, gt1, g2, sh2, sc2, wr)


def _gla_in_kernel(x_ref, g_ref, sh_ref, sc_ref, w_ref, wl_ref, wg_ref, bg_ref,
                   q_ref, k_ref, v_ref, r_ref, gf_ref, gb_ref, *, dk, dv, qscale):
    hn = _norm_mod(x_ref[0], g_ref[...], sc_ref[0], sh_ref[0]).astype(BF16)
    q_ref[0] = _bdot(hn, w_ref[:, 0:dk]) * qscale
    k_ref[0] = _bdot(hn, w_ref[:, dk:2 * dk])
    v_ref[0] = _bdot(hn, w_ref[:, 2 * dk:2 * dk + dv])
    r_ref[0] = _bdot(hn, w_ref[:, 2 * dk + dv:2 * dk + 2 * dv])
    low = _bdot(hn, wl_ref[...])
    gates = _log_sigmoid(_dot3(low, wg_ref[0], wg_ref[1]) + bg_ref[...]) * (1.0 / GLA_TAU)
    gf_ref[0] = gates[:, 0:dk]
    gb_ref[0] = gates[:, dk:2 * dk]


def _gla_in(x, g, sh, sc, w_main_bf, w_low_bf, w_gate, b_gate, dk, dv, tm=512):
    b, s, d = x.shape
    tm = min(tm, s)
    qscale = float((dk // GLA_HEADS) ** -0.5)
    kern = functools.partial(_gla_in_kernel, dk=dk, dv=dv, qscale=qscale)
    tokk = pl.BlockSpec((1, tm, dk), lambda i, j: (i, j, 0))
    tokv = pl.BlockSpec((1, tm, dv), lambda i, j: (i, j, 0))
    return pl.pallas_call(
        kern,
        out_shape=(jax.ShapeDtypeStruct((b, s, dk), F32),
                   jax.ShapeDtypeStruct((b, s, dk), F32),
                   jax.ShapeDtypeStruct((b, s, dv), F32),
                   jax.ShapeDtypeStruct((b, s, dv), F32),
                   jax.ShapeDtypeStruct((b, s, dk), F32),
                   jax.ShapeDtypeStruct((b, s, dk), F32)),
        grid=(b, s // tm),
        in_specs=[
            pl.BlockSpec((1, tm, d), lambda i, j: (i, j, 0)),
            pl.BlockSpec((1, d), lambda i, j: (0, 0)),
            pl.BlockSpec((1, 1, d), lambda i, j: (i, 0, 0)),
            pl.BlockSpec((1, 1, d), lambda i, j: (i, 0, 0)),
            pl.BlockSpec((d, 2 * dk + 2 * dv), lambda i, j: (0, 0)),
            pl.BlockSpec((d, LANES), lambda i, j: (0, 0)),
            pl.BlockSpec((2, LANES, 2 * dk), lambda i, j: (0, 0, 0)),
            pl.BlockSpec((1, 2 * dk), lambda i, j: (0, 0)),
        ],
        out_specs=(tokk, tokk, tokv, tokv, tokk, tokk),
        compiler_params=_params(("parallel", "parallel"), 56),
        name="gla_in",
    )(x, g, sh, sc, w_main_bf, w_low_bf, w_gate, b_gate)


def _gla_core_kernel(q_ref, k_ref, v_ref, gf_ref, gb_ref, o_ref, sf_ref, sb_ref,
                     *, ngroup):
    C = GLA_CHUNK
    G = GLA_GROUP
    R = C * G
    ri = lax.broadcasted_iota(I32, (C, C), 0)
    ci = lax.broadcasted_iota(I32, (C, C), 1)
    mask_f = ci <= ri
    mask_b = ci > ri
    rowc = lax.broadcasted_iota(I32, (R, 1), 0) % C
    sf_ref[...] = jnp.zeros_like(sf_ref)
    sb_ref[...] = jnp.zeros_like(sb_ref)
    nt = (((1,), (1,)), ((), ()))
    tn = (((0,), (0,)), ((), ()))

    def chunk_cumsum(x, reverse):
        sh = 1
        while sh < C:
            if reverse:
                x = x + jnp.where(rowc < C - sh, pltpu.roll(x, R - sh, 0), 0.0)
            else:
                x = x + jnp.where(rowc >= sh, pltpu.roll(x, sh, 0), 0.0)
            sh *= 2
        return x

    def group(gi, g_ref, s_ref, reverse, accumulate):
        rows = pl.ds(pl.multiple_of(gi * R, R), R)
        q = q_ref[0, rows, :]
        k = k_ref[0, rows, :]
        v = v_ref[0, rows, :].astype(BF16)
        cum = chunk_cumsum(g_ref[0, rows, :], reverse)
        q_dec = (q * jnp.exp(cum)).astype(BF16)
        k_inv = (k * jnp.exp(-cum)).astype(BF16)
        st = s_ref[...]
        outs = [None] * G
        for n in (range(G - 1, -1, -1) if reverse else range(G)):
            sl = slice(n * C, (n + 1) * C)
            edge = n * C if reverse else n * C + C - 1
            tot = cum[edge:edge + 1, :]
            k_end = (k[sl] * jnp.exp(tot - cum[sl])).astype(BF16)
            att = lax.dot_general(q_dec[sl], k_inv[sl], nt, preferred_element_type=F32)
            att = jnp.where(mask_b if reverse else mask_f, att, 0.0).astype(BF16)
            outs[n] = _bdot(att, v[sl]) + lax.dot_general(
                q_dec[sl], st.astype(BF16), nt, preferred_element_type=F32)
            st = st * jnp.exp(tot) + lax.dot_general(
                v[sl], k_end, tn, preferred_element_type=F32)
        s_ref[...] = st
        o = jnp.concatenate(outs, axis=0)
        if accumulate:
            o_ref[0, rows, :] += o
        else:
            o_ref[0, rows, :] = o

    def first_half(i, carry):
        group(i, gf_ref, sf_ref, False, False)
        group(ngroup - 1 - i, gb_ref, sb_ref, True, False)
        return carry

    def second_half(i, carry):
        group(i, gf_ref, sf_ref, False, True)
        group(ngroup - 1 - i, gb_ref, sb_ref, True, True)
        return carry

    half = ngroup // 2
    lax.fori_loop(0, half, first_half, 0)
    lax.fori_loop(half, ngroup, second_half, 0)


def _gla_core(q, k, v, gf, gb):
    b, s, dk = q.shape
    dv = v.shape[-1]
    hk = dk // GLA_HEADS
    hv = dv // GLA_HEADS
    ngroup = s // (GLA_CHUNK * GLA_GROUP)
    assert ngroup % 2 == 0
    kern = functools.partial(_gla_core_kernel, ngroup=ngroup)
    speck = pl.BlockSpec((1, s, hk), lambda i, h: (i, 0, h))
    specv = pl.BlockSpec((1, s, hv), lambda i, h: (i, 0, h))
    return pl.pallas_call(
        kern,
        out_shape=jax.ShapeDtypeStruct((b, s, dv), F32),
        grid=(b, GLA_HEADS),
        in_specs=[speck, speck, specv, speck, speck],
        out_specs=specv,
        scratch_shapes=[pltpu.VMEM((hv, hk), F32), pltpu.VMEM((hv, hk), F32)],
        compiler_params=_params(("parallel", "parallel"), 48),
        name="gla_core",
    )(q, k, v, gf, gb)


def _gla_out_kernel(o_ref, r_ref, ng_ref, w_ref, x_ref, gt_ref, g2_ref, sh2_ref,
                    sc2_ref, wr_ref, xo_ref, hn_ref, lg_ref, *, hv):
    o = o_ref[0]
    parts = []
    for h in range(GLA_HEADS):
        oh = o[:, h * hv:(h + 1) * hv]
        ms = jnp.mean(oh * oh, axis=-1, keepdims=True)
        parts.append(oh * lax.rsqrt(ms + EPS))
    on = jnp.concatenate(parts, axis=-1) * ng_ref[...]
    a = on * _silu(r_ref[0])
    _mixer_tail(a.astype(BF16), w_ref, x_ref, gt_ref, g2_ref, sh2_ref, sc2_ref,
                wr_ref, xo_ref, hn_ref, lg_ref)


def _gla_out(o, r, ng, w_bf, x, gt1, g2, sh2, sc2, wr, tm=512):
    b, s, d = x.shape
    dv = o.shape[-1]
    tm = min(tm, s)
    tok, vec, bvec, wsq, wrs, out_shape, out_specs = _tail_specs(b, s, d, tm)
    tokv = pl.BlockSpec((1, tm, dv), lambda i, j: (i, j, 0))
    kern = functools.partial(_gla_out_kernel, hv=dv // GLA_HEADS)
    return pl.pallas_call(
        kern,
        out_shape=out_shape,
        grid=(b, s // tm),
        in_specs=[tokv, tokv, pl.BlockSpec((1, dv), lambda i, j: (0, 0)),
                  pl.BlockSpec((dv, d), lambda i, j: (0, 0)),
                  tok, bvec, vec, bvec, bvec, wrs],
        out_specs=out_specs,
        compiler_params=_params(("parallel", "parallel"), 48),
        name="gla_out",
    )(o, r, ng, w_bf, x, gt1, g2, sh2, sc2, wr)


def _prefix_sum_lanes(x, n):
    lane = lax.broadcasted_iota(I32, x.shape, 1)
    sh = 1
    while sh < n:
        x = x + jnp.where(lane >= sh, pltpu.roll(x, sh, 1), 0)
        sh *= 2
    return x


def _route_kernel(lg_ref, slot_t_ref, slot_ref, aff_ref, toff_ref, *, cap, seq):
    E = N_EXPERTS
    lg = lg_ref[0]
    lane = lax.broadcasted_iota(I32, lg.shape, 1)
    live = lane < E
    lgm = jnp.where(live, lg, -jnp.inf)
    mx = jnp.max(lgm, axis=-1, keepdims=True)
    ex = jnp.where(live, jnp.exp(lgm - mx), 0.0)
    aff = ex / jnp.sum(ex, axis=-1, keepdims=True)
    aff_t = aff.T[0:E, :]
    aff_ref[0] = aff_t
    bits = pltpu.bitcast(aff_t, I32)

    def search(_, lohi):
        lo, hi = lohi
        mid = lo + (hi - lo + 1) // 2
        cnt = jnp.sum((bits >= mid).astype(I32), axis=1, keepdims=True)
        ok = cnt >= cap
        return jnp.where(ok, mid, lo), jnp.where(ok, hi, mid - 1)

    lo0 = jnp.zeros((E, 1), I32)
    hi0 = jnp.full((E, 1), 0x7F800000, I32)
    thr, _ = lax.fori_loop(0, 32, search, (lo0, hi0))
    gt = bits > thr
    eq = bits == thr
    need = cap - jnp.sum(gt.astype(I32), axis=1, keepdims=True)
    eq_i = eq.astype(I32)
    eq_rank = _prefix_sum_lanes(eq_i, seq) - eq_i
    sel = jnp.logical_or(gt, jnp.logical_and(eq, eq_rank < need))
    sel_i = sel.astype(I32)
    pos = _prefix_sum_lanes(sel_i, seq) - sel_i
    slot_t = jnp.where(sel, pos, -1)
    slot_t_ref[0] = slot_t
    pad = jnp.full((LANES - E, seq), -1, I32)
    slot_ref[0] = jnp.concatenate([slot_t, pad], axis=0).T
    lane_e = lax.broadcasted_iota(I32, (E, LANES), 1)
    run = jnp.zeros((E, 1), I32)
    toff = jnp.zeros((E, LANES), I32)
    for j in range(seq // MOE_TILE):
        run = run + jnp.sum(sel_i[:, j * MOE_TILE:(j + 1) * MOE_TILE], axis=1, keepdims=True)
        toff = toff + jnp.where(lane_e == j + 1, run, 0)
    toff_ref[0] = toff


def _route(logits, cap):
    b, s, _ = logits.shape
    assert s // MOE_TILE < LANES
    kern = functools.partial(_route_kernel, cap=cap, seq=s)
    return pl.pallas_call(
        kern,
        out_shape=(jax.ShapeDtypeStruct((b, N_EXPERTS, s), I32),
                   jax.ShapeDtypeStruct((b, s, LANES), I32),
                   jax.ShapeDtypeStruct((b, N_EXPERTS, s), F32),
                   jax.ShapeDtypeStruct((b, N_EXPERTS, LANES), I32)),
        grid=(b,),
        in_specs=[pl.BlockSpec((1, s, LANES), lambda i: (i, 0, 0))],
        out_specs=(pl.BlockSpec((1, N_EXPERTS, s), lambda i: (i, 0, 0)),
                   pl.BlockSpec((1, s, LANES), lambda i: (i, 0, 0)),
                   pl.BlockSpec((1, N_EXPERTS, s), lambda i: (i, 0, 0)),
                   pl.BlockSpec((1, N_EXPERTS, LANES), lambda i: (i, 0, 0))),
        compiler_params=_params(("parallel",), 48),
        name="moe_route",
    )(logits)


def _tile_range(toff_ref, base, nt, lo, hi):
    jlo = jnp.int32(0)
    jend = jnp.int32(0)
    for j in range(nt):
        jlo = jlo + jnp.where(toff_ref[base + j + 1] <= lo, 1, 0)
        jend = jend + jnp.where(toff_ref[base + j] < hi, 1, 0)
    return jlo, jend


def _moe_ffn_kernel(toff_ref, hn_ref, slot_ref, aff_ref, wg_ref, wu_ref, wd_ref,
                    y_ref, xg_ref, acc_ref, gacc_ref, gate_ref, *, cap, seq):
    f = pl.program_id(2)
    nf = pl.num_programs(2)

    @pl.when(f == 0)
    def _():
        base = (pl.program_id(0) * pl.num_programs(1) + pl.program_id(1)) * LANES
        nt = seq // MOE_TILE
        wt = min(MOE_WIN_TILES, nt)
        r = lax.broadcasted_iota(I32, (MOE_SLOT_BLOCK, 1), 0)

        def gathered(t0, width, r0):
            hit = slot_ref[0, 0, :, pl.ds(t0, width)] == r + r0
            onehot = jnp.where(hit, 1.0, 0.0).astype(BF16)
            gate = jnp.sum(jnp.where(hit, aff_ref[0, 0, :, pl.ds(t0, width)], 0.0),
                           axis=1, keepdims=True)
            return _bdot(onehot, hn_ref[0, pl.ds(t0, width), :]), gate

        for kb in range(cap // MOE_SLOT_BLOCK):
            r0 = kb * MOE_SLOT_BLOCK
            rows = slice(r0, r0 + MOE_SLOT_BLOCK)
            jlo, jend = _tile_range(toff_ref, base, nt, r0, r0 + MOE_SLOT_BLOCK)
            jw = jnp.minimum(jlo, nt - wt)
            xw, gw = gathered(pl.multiple_of(jw * MOE_TILE, MOE_TILE), wt * MOE_TILE, r0)
            gacc_ref[...] = xw
            gate_ref[rows, :] = gw

            def tile(j, carry, rows=rows, r0=r0):
                xj, gj = gathered(pl.multiple_of(j * MOE_TILE, MOE_TILE), MOE_TILE, r0)
                gacc_ref[...] += xj
                gate_ref[rows, :] += gj
                return carry

            lax.fori_loop(jnp.maximum(jlo, jw + wt), jend, tile, 0)
            xg_ref[rows, :] = gacc_ref[...].astype(BF16)
        acc_ref[...] = jnp.zeros_like(acc_ref)

    xg = xg_ref[...]
    a = _bdot(xg, wg_ref[0])
    u = _bdot(xg, wu_ref[0])
    hh = (_silu(a) * u).astype(BF16)
    acc_ref[...] += _bdot(hh, wd_ref[0])

    @pl.when(f == nf - 1)
    def _():
        y_ref[0, 0] = (acc_ref[...] * gate_ref[...]).astype(BF16)


def _moe_ffn(hn_bf, slot_t, aff_t, toff, wg_bf, wu_bf, wd_bf, cap, fc=1024):
    b, s, d = hn_bf.shape
    e, _, fdim = wg_bf.shape
    fc = min(fc, fdim)
    assert cap % MOE_SLOT_BLOCK == 0 and s % MOE_TILE == 0
    kern = functools.partial(_moe_ffn_kernel, cap=cap, seq=s)
    grid_spec = pltpu.PrefetchScalarGridSpec(
        num_scalar_prefetch=1,
        grid=(b, e, fdim // fc),
        in_specs=[
            pl.BlockSpec((1, s, d), lambda i, j, f, t: (i, 0, 0)),
            pl.BlockSpec((1, 1, 1, s), lambda i, j, f, t: (i, j, 0, 0)),
            pl.BlockSpec((1, 1, 1, s), lambda i, j, f, t: (i, j, 0, 0)),
            pl.BlockSpec((1, d, fc), lambda i, j, f, t: (j, 0, f)),
            pl.BlockSpec((1, d, fc), lambda i, j, f, t: (j, 0, f)),
            pl.BlockSpec((1, fc, d), lambda i, j, f, t: (j, f, 0)),
        ],
        out_specs=pl.BlockSpec((1, 1, cap, d), lambda i, j, f, t: (i, j, 0, 0)),
        scratch_shapes=[pltpu.VMEM((cap, d), BF16), pltpu.VMEM((cap, d), F32),
                        pltpu.VMEM((MOE_SLOT_BLOCK, d), F32), pltpu.VMEM((cap, 1), F32)],
    )
    return pl.pallas_call(
        kern,
        out_shape=jax.ShapeDtypeStruct((b, e, cap, d), BF16),
        grid_spec=grid_spec,
        compiler_params=_params(("parallel", "arbitrary", "arbitrary"), 56),
        name="moe_ffn",
    )(toff.reshape(-1), hn_bf, slot_t.reshape(b, e, 1, s), aff_t.reshape(b, e, 1, s),
      wg_bf, wu_bf, wd_bf)


def _moe_combine_kernel(toff_ref, x_ref, gt_ref, slot_ref, y_ref, gfin_ref, o_ref,
                        ycat_ref, oh_ref, acc_ref, *, cap, final, tt, ne):
    nsub = tt // MOE_TILE
    kw = min(MOE_TILE, cap)
    slot = slot_ref[0]
    unfit = []
    for e in range(ne):
        base = (pl.program_id(0) * ne + e) * LANES + pl.program_id(1) * nsub
        lo = toff_ref[base]
        hi = toff_ref[base + nsub]
        start = jnp.minimum((lo // BF16_ROWS) * BF16_ROWS, cap - kw)
        s0 = pl.multiple_of(start, BF16_ROWS)
        fits = hi <= start + kw
        unfit.append(jnp.logical_not(fits))
        ycat_ref[e * kw:(e + 1) * kw, :] = y_ref[0, e, pl.ds(s0, kw), :]
        r = jnp.where(fits, lax.broadcasted_iota(I32, (1, kw), 1) + s0, -2)
        hit = slot[:, e:e + 1] == r
        oh_ref[:, e * kw:(e + 1) * kw] = jnp.where(hit, 1.0, 0.0).astype(BF16)
    acc_ref[...] = _bdot(oh_ref[...], ycat_ref[...])

    for e in range(ne):

        @pl.when(unfit[e])
        def _(e=e):
            r = lax.broadcasted_iota(I32, (1, cap), 1)
            onehot = jnp.where(slot[:, e:e + 1] == r, 1.0, 0.0).astype(BF16)
            acc_ref[...] += _bdot(onehot, y_ref[0, e])

    xn = x_ref[0] + gt_ref[0] * acc_ref[...]
    if final:
        ms = jnp.mean(xn * xn, axis=-1, keepdims=True)
        xn = xn * lax.rsqrt(ms + EPS) * gfin_ref[...]
    o_ref[0] = xn


def _moe_combine(x, gt2, slot, toff, y, gfin, cap, final, tt=512):
    b, s, d = x.shape
    e = y.shape[1]
    tt = min(tt, s)
    kw = min(MOE_TILE, cap)
    assert tt % MOE_TILE == 0 and cap % kw == 0
    kern = functools.partial(_moe_combine_kernel, cap=cap, final=final, tt=tt, ne=e)
    grid_spec = pltpu.PrefetchScalarGridSpec(
        num_scalar_prefetch=1,
        grid=(b, s // tt),
        in_specs=[
            pl.BlockSpec((1, tt, d), lambda i, t, o: (i, t, 0)),
            pl.BlockSpec((1, 1, d), lambda i, t, o: (i, 0, 0)),
            pl.BlockSpec((1, tt, LANES), lambda i, t, o: (i, t, 0)),
            pl.BlockSpec((1, e, cap, d), lambda i, t, o: (i, 0, 0, 0),
                         pipeline_mode=pl.Buffered(1)),
            pl.BlockSpec((1, d), lambda i, t, o: (0, 0)),
        ],
        out_specs=pl.BlockSpec((1, tt, d), lambda i, t, o: (i, t, 0)),
        scratch_shapes=[pltpu.VMEM((e * kw, d), BF16), pltpu.VMEM((tt, e * kw), BF16),
                        pltpu.VMEM((tt, d), F32)],
    )
    return pl.pallas_call(
        kern,
        out_shape=jax.ShapeDtypeStruct((b, s, d), F32),
        grid_spec=grid_spec,
        compiler_params=_params(("parallel", "arbitrary"), 56),
        name="moe_combine",
    )(toff.reshape(-1), x, gt2, slot, y, gfin)


def _hi_lo(w):
    hi = w.astype(BF16)
    return jnp.stack([hi, (w - hi.astype(F32)).astype(BF16)])


def kernel(x, c, w_ada, b_ada, norm_mix, norm_ffn, hy_w_in, hy_conv_w, hy_conv_b, hy_f_w1, hy_f_b1, hy_f_w2, hy_f_b2, hy_f_w3, hy_f_freq, hy_skip, hy_w_out, gla_w_in, gla_w_gf, gla_b_gf, gla_w_gb, gla_b_gb, gla_norm, gla_w_out, moe_router, moe_w_gate, moe_w_up, moe_w_down, norm_final):
    b, s, d = x.shape
    depth = w_ada.shape[0]
    n_exp = moe_router.shape[-1]
    cap = EC_CAPACITY * s // n_exp
    dk = gla_w_gf.shape[-1]
    dv = gla_w_out.shape[1]
    nb = s // CONV_BLOCK

    mod = _adaln(c, w_ada, b_ada)
    for i in range(depth):
        j = i // 2
        sh1, sc1, gt1, sh2, sc2, gt2 = (mod[i, m].reshape(b, 1, d) for m in range(N_MOD))
        g1 = norm_mix[i][None, :]
        g2 = norm_ffn[i][None, :]
        wr = _hi_lo(jnp.zeros((d, LANES), F32).at[:, :n_exp].set(moe_router[i]))
        if i % 2 == 0:
            x0, u = _hy_in(x, g1, sh1, sc1, hy_w_in[j].astype(BF16), hy_conv_w[j],
                           hy_conv_b[j][None, :])
            gt, nrm = _hy_filter(s, hy_f_w1[j], hy_f_b1[j], hy_f_w2[j], hy_f_b2[j],
                                 hy_f_w3[j], hy_f_freq[j])
            ut = u.reshape(b, nb, CONV_BLOCK, d).transpose(3, 1, 0, 2)
            yt = _hy_conv(gt, nrm, ut.reshape(d, nb * b, CONV_BLOCK), nb, b)
            y = yt.reshape(d, nb, b, CONV_BLOCK).transpose(2, 1, 3, 0).reshape(b, s, d)
            x, hn, lg = _hy_out(y, u, x0, hy_skip[j][None, :], hy_w_out[j].astype(BF16),
                                x, gt1, g2, sh2, sc2, wr)
        else:
            w_in = gla_w_in[j]
            w_low = jnp.zeros((d, LANES), F32).at[:, :2 * GLA_GATE_RANK].set(
                w_in[:, 2 * dk + 2 * dv:]).astype(BF16)
            w_gate = jnp.zeros((LANES, 2 * dk), F32)
            w_gate = w_gate.at[:GLA_GATE_RANK, :dk].set(gla_w_gf[j])
            w_gate = w_gate.at[GLA_GATE_RANK:2 * GLA_GATE_RANK, dk:].set(gla_w_gb[j])
            b_gate = jnp.concatenate([gla_b_gf[j], gla_b_gb[j]])[None, :]
            q, k, v, r, gf, gb = _gla_in(x, g1, sh1, sc1,
                                         w_in[:, :2 * dk + 2 * dv].astype(BF16),
                                         w_low, _hi_lo(w_gate), b_gate, dk, dv)
            o = _gla_core(q, k, v, gf, gb)
            x, hn, lg = _gla_out(o, r, gla_norm[j].reshape(1, dv),
                                 gla_w_out[j].astype(BF16), x, gt1, g2, sh2, sc2, wr)
        slot_t, slot, aff_t, toff = _route(lg, cap)
        y = _moe_ffn(hn, slot_t, aff_t, toff, moe_w_gate[i].astype(BF16),
                     moe_w_up[i].astype(BF16), moe_w_down[i].astype(BF16), cap)
        x = _moe_combine(x, gt2, slot, toff, y, norm_final[None, :], cap,
                         final=(i == depth - 1))
    return x
```

```python
import functools
import math

import jax
import jax.numpy as jnp
import numpy as np
from jax import lax
from jax.experimental import pallas as pl
from jax.experimental.pallas import tpu as pltpu

F32 = jnp.float32
BF16 = jnp.bfloat16
I32 = jnp.int32
HIGHEST = lax.Precision.HIGHEST

EPS = 1e-6
N_MOD = 6
LANES = 128
VMEM_LIMIT_CAP = 56 << 20

SHORT_CONV = 3
FILTER_BANDS = 16
DECAY_TARGET = 1e-2
FAST_DECAY_PCT = 0.3
SLOW_DECAY_PCT = 1.5
MIN_DECAY = math.log(DECAY_TARGET) / SLOW_DECAY_PCT
MAX_DECAY = math.log(DECAY_TARGET) / FAST_DECAY_PCT
CONV_BLOCK = 256
CONV_CH = 4


GLA_HEADS = 4
GLA_GATE_RANK = 16
GLA_TAU = 16.0
GLA_CHUNK = 64
GLA_GROUP = 16

N_EXPERTS = 16
EC_CAPACITY = 2
MOE_TILE = 256
MOE_SLOT_BLOCK = 128
MOE_WIN_TILES = 6
BF16_ROWS = 16


def _params(sem, vmem_mb):
    return pltpu.CompilerParams(
        dimension_semantics=sem,
        vmem_limit_bytes=min(vmem_mb << 20, VMEM_LIMIT_CAP))


def _sigmoid(x):
    return 1.0 / (1.0 + jnp.exp(-x))


def _silu(x):
    return x * _sigmoid(x)


def _log_sigmoid(x):
    return jnp.minimum(x, 0.0) - jnp.log(1.0 + jnp.exp(-jnp.abs(x)))


def _norm_mod(x, g, sc, sh):
    ms = jnp.mean(x * x, axis=-1, keepdims=True)
    return (x * lax.rsqrt(ms + EPS) * g) * (1.0 + sc) + sh


def _bdot(a, b):
    return jnp.dot(a, b, preferred_element_type=F32)


def _hdot(a, b):
    return jnp.dot(a, b, preferred_element_type=F32, precision=HIGHEST)


def _split_bf16(a):
    hi = a.astype(BF16)
    return hi, (a - hi.astype(F32)).astype(BF16)


def _dot3(a, b_hi, b_lo):
    a_hi, a_lo = _split_bf16(a)
    return _bdot(a_hi, b_hi) + _bdot(a_hi, b_lo) + _bdot(a_lo, b_hi)


def _adaln_kernel(c_ref, w_ref, b_ref, o_ref):
    cond = _silu(c_ref[...])
    o_ref[0, 0] = _hdot(cond, w_ref[0]) + b_ref[0]


def _adaln(c, w_ada, b_ada):
    depth, d, _ = w_ada.shape
    b = c.shape[0]
    return pl.pallas_call(
        _adaln_kernel,
        out_shape=jax.ShapeDtypeStruct((depth, N_MOD, b, d), F32),
        grid=(depth, N_MOD),
        in_specs=[
            pl.BlockSpec((b, d), lambda i, n: (0, 0)),
            pl.BlockSpec((1, d, d), lambda i, n: (i, 0, n)),
            pl.BlockSpec((1, 1, d), lambda i, n: (i, 0, n)),
        ],
        out_specs=pl.BlockSpec((1, 1, b, d), lambda i, n: (i, n, 0, 0)),
        compiler_params=_params(("parallel", "parallel"), 32),
        name="adaln_mod",
    )(c, w_ada, b_ada.reshape(depth, 1, N_MOD * d))


def _hy_in_kernel(x_ref, xp_ref, xn_ref, g_ref, sh_ref, sc_ref, w_ref, cw_ref,
                  cb_ref, x0_ref, u_ref, *, tm, d):
    s = pl.program_id(1)
    ns = pl.num_programs(1)
    g = g_ref[...]
    sh = sh_ref[0]
    sc = sc_ref[0]
    hn = _norm_mod(x_ref[0], g, sc, sh).astype(BF16)
    hp = _norm_mod(xp_ref[0], g, sc, sh).astype(BF16)
    hx = _norm_mod(xn_ref[0], g, sc, sh).astype(BF16)
    row = lax.broadcasted_iota(I32, (tm, 1), 0)
    outs = []
    for part in range(3):
        cols = slice(part * d, (part + 1) * d)
        w = w_ref[:, cols]
        p = _bdot(hn, w)
        pp = jnp.where(s > 0, _bdot(hp, w)[7:8, :], 0.0)
        pn = jnp.where(s < ns - 1, _bdot(hx, w)[0:1, :], 0.0)
        p_prev = jnp.where(row == 0, pp, pltpu.roll(p, 1, 0))
        p_next = jnp.where(row == tm - 1, pn, pltpu.roll(p, tm - 1, 0))
        cw = cw_ref[:, cols]
        outs.append(cw[0:1] * p_prev + cw[1:2] * p + cw[2:3] * p_next
                    + cb_ref[:, cols])
    x0_ref[0] = outs[0]
    u_ref[0] = outs[2] * outs[1]


def _hy_in(x, g, sh, sc, w_bf, conv_w, conv_b, tm=512):
    b, s, d = x.shape
    tm = min(tm, s)
    hb = tm // 8
    nhb = s // 8
    kern = functools.partial(_hy_in_kernel, tm=tm, d=d)
    return pl.pallas_call(
        kern,
        out_shape=(jax.ShapeDtypeStruct((b, s, d), F32),
                   jax.ShapeDtypeStruct((b, s, d), F32)),
        grid=(b, s // tm),
        in_specs=[
            pl.BlockSpec((1, tm, d), lambda i, j: (i, j, 0)),
            pl.BlockSpec((1, 8, d), lambda i, j: (i, jnp.maximum(j * hb - 1, 0), 0)),
            pl.BlockSpec((1, 8, d), lambda i, j: (i, jnp.minimum((j + 1) * hb, nhb - 1), 0)),
            pl.BlockSpec((1, d), lambda i, j: (0, 0)),
            pl.BlockSpec((1, 1, d), lambda i, j: (i, 0, 0)),
            pl.BlockSpec((1, 1, d), lambda i, j: (i, 0, 0)),
            pl.BlockSpec((d, 3 * d), lambda i, j: (0, 0)),
            pl.BlockSpec((SHORT_CONV, 3 * d), lambda i, j: (0, 0)),
            pl.BlockSpec((1, 3 * d), lambda i, j: (0, 0)),
        ],
        out_specs=(pl.BlockSpec((1, tm, d), lambda i, j: (i, j, 0)),
                   pl.BlockSpec((1, tm, d), lambda i, j: (i, j, 0))),
        compiler_params=_params(("parallel", "parallel"), 56),
        name="hyena_in",
    )(x, x, x, g, sh, sc, w_bf, conv_w, conv_b)


FILT_T_COL = 127
FILT_VALID_COL = 126


def _hy_filter_kernel(z_ref, w1_ref, b1_ref, w2_ref, b2_ref, w3_ref, fr_ref,
                      dl_ref, gt_ref, nrm_ref):
    xb = pl.program_id(0)
    z = z_ref[...]
    fr = fr_ref[...]
    h = jnp.sin(fr * (_hdot(z, w1_ref[...]) + b1_ref[...]))
    h = jnp.sin(fr * (_hdot(h, w2_ref[...]) + b2_ref[...]))
    h = _hdot(h, w3_ref[0])
    t = z[:, FILT_T_COL:FILT_T_COL + 1]
    valid = z[:, FILT_VALID_COL:FILT_VALID_COL + 1]
    h = h * jnp.exp(-t * dl_ref[...]) * valid
    ht = h.T
    gt_ref[...] = ht
    part = jnp.sum(jnp.abs(ht), axis=1, keepdims=True)

    @pl.when(xb == 0)
    def _():
        nrm_ref[...] = jnp.zeros_like(nrm_ref)

    nrm_ref[...] += jnp.broadcast_to(part, nrm_ref.shape)


@functools.lru_cache(maxsize=None)
def _filter_positions(L, emb):
    f32 = np.float32
    t = np.linspace(0.0, 1.0, L, dtype=f32)
    bands = np.linspace(1e-4, FILTER_BANDS - 1, FILTER_BANDS, dtype=f32)
    ang = f32(2.0 * math.pi / L) * np.arange(L, dtype=f32)[:, None] * bands[None, :]
    zf = np.concatenate([t[:, None], np.cos(ang), -np.sin(ang)], axis=-1).astype(f32)
    assert zf.shape[1] == emb
    lag = np.abs(np.arange(2 * L) - L)
    lagc = np.minimum(lag, L - 1)
    z2 = np.zeros((2 * L, LANES), f32)
    z2[:, :emb] = zf[lagc]
    z2[:, FILT_T_COL] = t[lagc]
    z2[:, FILT_VALID_COL] = (lag < L).astype(f32)
    return z2


def _hy_filter(seq, w1, b1, w2, b2, w3, freq, xt=512):
    L = seq
    d = w3.shape[-1] // 2
    hid = w2.shape[0]
    emb = w1.shape[0]
    xt = min(xt, L)
    z2 = jnp.asarray(_filter_positions(L, emb))
    w1p = jnp.zeros((LANES, hid), F32).at[:emb].set(w1)
    w3d = w3.reshape(hid, 2, d).transpose(1, 0, 2)
    deltas = jnp.abs(jnp.linspace(MIN_DECAY, MAX_DECAY, d, dtype=F32))[None, :]
    nxb = 2 * L // xt
    half = nxb // 2
    return pl.pallas_call(
        _hy_filter_kernel,
        out_shape=(jax.ShapeDtypeStruct((d, 2 * L), F32),
                   jax.ShapeDtypeStruct((d, LANES), F32)),
        grid=(nxb,),
        in_specs=[
            pl.BlockSpec((xt, LANES), lambda i: (i, 0)),
            pl.BlockSpec((LANES, hid), lambda i: (0, 0)),
            pl.BlockSpec((1, hid), lambda i: (0, 0)),
            pl.BlockSpec((hid, hid), lambda i: (0, 0)),
            pl.BlockSpec((1, hid), lambda i: (0, 0)),
            pl.BlockSpec((1, hid, d), lambda i: (jnp.where(i >= half, 0, 1), 0, 0)),
            pl.BlockSpec((1, hid), lambda i: (0, 0)),
            pl.BlockSpec((1, d), lambda i: (0, 0)),
        ],
        out_specs=(pl.BlockSpec((d, xt), lambda i: (0, i)),
                   pl.BlockSpec((d, LANES), lambda i: (0, 0))),
        compiler_params=_params(("arbitrary",), 32),
        name="hyena_filter",
    )(z2, w1p, b1[None, :], w2, b2[None, :], w3d, freq[None, :], deltas)


def _hy_conv_kernel(g_ref, nrm_ref, u_ref, y_ref, pg_ref, strip_ref, ucat_ref,
                    *, nb, bsz):
    P = CONV_BLOCK
    nshift = 2 * nb - 1
    nwin = 2 * nb * P // LANES
    ngrp = nshift * P // LANES
    zoff = nshift * P // LANES
    HR = LANES // 2
    hb = nb // 2
    nsh = nb + hb - 1
    hrows = hb * bsz

    @pl.when(pl.program_id(0) == 0)
    def _():
        ucat_ref[...] = jnp.zeros_like(ucat_ref)

    row = lax.broadcasted_iota(I32, (HR, LANES), 0)
    col = lax.broadcasted_iota(I32, (HR, LANES), 1)
    upper = col >= 2 * row
    nq = P // LANES

    for c in range(CONV_CH):
        g = g_ref[c]
        gb = pltpu.bitcast(g.astype(BF16).astype(F32), I32)
        gm1 = pltpu.bitcast(pltpu.roll(g, 1, 1).astype(BF16).astype(F32), I32)
        pg_ref[c] = jnp.bitwise_or(jnp.bitwise_and(jnp.right_shift(gb, 16), 0xFFFF),
                                   jnp.bitwise_and(gm1, -65536))

        def rolled(w, c=c):
            pw = pg_ref[c, :, w * LANES:(w + 1) * LANES]
            return pltpu.roll(jnp.broadcast_to(pw, (HR, LANES)), 0, 1, stride=2,
                              stride_axis=0)

        cur = rolled(nwin - 1)
        for w in range(nwin - 1, 0, -1):
            prev = rolled(w - 1)
            blk = pltpu.bitcast(jnp.where(upper, cur, prev), BF16)
            for q in range(nq):
                m = zoff + q - w
                if 0 <= m < ngrp:
                    strip_ref[c, m * LANES:(m + 1) * LANES, q * LANES:(q + 1) * LANES] = blk
            cur = prev

        u = u_ref[c].astype(BF16)
        for m in range(nsh):
            il0 = max(0, hb - 1 - m)
            il1 = min(hb, nb + hb - 1 - m)
            j0 = il0 + m - (hb - 1)
            ucat_ref[c, il0 * bsz:il1 * bsz, m * P:(m + 1) * P] = (
                u[j0 * bsz:(j0 + il1 - il0) * bsz])
        inv = 1.0 / nrm_ref[c, :, 0:1]
        for h in range(2):
            off = (nb - 1 - (h * hb + hb - 1)) * P
            y = _bdot(ucat_ref[c], strip_ref[c, off:off + nsh * P, :])
            y_ref[c, h * hrows:(h + 1) * hrows, :] = y * inv


def _hy_conv(gt, nrm, ut, nb, bsz):
    d = ut.shape[0]
    P = CONV_BLOCK
    rows = nb * bsz
    nshift = 2 * nb - 1
    ch = CONV_CH
    assert nb % 2 == 0 and d % ch == 0
    kern = functools.partial(_hy_conv_kernel, nb=nb, bsz=bsz)
    return pl.pallas_call(
        kern,
        out_shape=jax.ShapeDtypeStruct((d, rows, P), F32),
        grid=(d // ch,),
        in_specs=[
            pl.BlockSpec((ch, 1, 2 * nb * P), lambda i: (i, 0, 0)),
            pl.BlockSpec((ch, 1, LANES), lambda i: (i, 0, 0)),
            pl.BlockSpec((ch, rows, P), lambda i: (i, 0, 0)),
        ],
        out_specs=pl.BlockSpec((ch, rows, P), lambda i: (i, 0, 0)),
        scratch_shapes=[
            pltpu.VMEM((ch, 1, 2 * nb * P), I32),
            pltpu.VMEM((ch, nshift * P, P), BF16),
            pltpu.VMEM((ch, rows // 2, (nb + nb // 2 - 1) * P), BF16),
        ],
        compiler_params=_params(("arbitrary",), 48),
        name="hyena_conv",
    )(gt.reshape(d, 1, 2 * nb * P), nrm.reshape(d, 1, LANES), ut)


def _mixer_tail(a_bf, w_ref, x_ref, gt_ref, g2_ref, sh2_ref, sc2_ref, wr_ref,
                xo_ref, hn_ref, lg_ref):
    xn = x_ref[0] + gt_ref[0] * _bdot(a_bf, w_ref[...])
    xo_ref[0] = xn
    hn = _norm_mod(xn, g2_ref[...], sc2_ref[0], sh2_ref[0])
    hn_ref[0] = hn.astype(BF16)
    lg_ref[0] = _dot3(hn, wr_ref[0], wr_ref[1])


def _hy_out_kernel(y_ref, u_ref, x0_ref, skip_ref, w_ref, x_ref, gt_ref, g2_ref,
                   sh2_ref, sc2_ref, wr_ref, xo_ref, hn_ref, lg_ref):
    a = (y_ref[0] + u_ref[0] * skip_ref[...]) * x0_ref[0]
    _mixer_tail(a.astype(BF16), w_ref, x_ref, gt_ref, g2_ref, sh2_ref, sc2_ref,
                wr_ref, xo_ref, hn_ref, lg_ref)


def _tail_specs(b, s, d, tm):
    tok = pl.BlockSpec((1, tm, d), lambda i, j: (i, j, 0))
    vec = pl.BlockSpec((1, d), lambda i, j: (0, 0))
    bvec = pl.BlockSpec((1, 1, d), lambda i, j: (i, 0, 0))
    wsq = pl.BlockSpec((d, d), lambda i, j: (0, 0))
    wr = pl.BlockSpec((2, d, LANES), lambda i, j: (0, 0, 0))
    out_shape = (jax.ShapeDtypeStruct((b, s, d), F32),
                 jax.ShapeDtypeStruct((b, s, d), BF16),
                 jax.ShapeDtypeStruct((b, s, LANES), F32))
    out_specs = (tok, tok, pl.BlockSpec((1, tm, LANES), lambda i, j: (i, j, 0)))
    return tok, vec, bvec, wsq, wr, out_shape, out_specs


def _hy_out(y, u, x0, skip, w_bf, x, gt1, g2, sh2, sc2, wr, tm=512):
    b, s, d = x.shape
    tm = min(tm, s)
    tok, vec, bvec, wsq, wrs, out_shape, out_specs = _tail_specs(b, s, d, tm)
    return pl.pallas_call(
        _hy_out_kernel,
        out_shape=out_shape,
        grid=(b, s // tm),
        in_specs=[tok, tok, tok, vec, wsq, tok, bvec, vec, bvec, bvec, wrs],
        out_specs=out_specs,
        compiler_params=_params(("parallel", "parallel"), 48),
        name="hyena_out",
    )(y, u, x0, skip, w_bf, x, gt1, g2, sh2, sc2, wr)


def _gla_in_kernel(x_ref, g_ref, sh_ref, sc_ref, w_ref, wl_ref, wg_ref, bg_ref,
                   q_ref, k_ref, v_ref, r_ref, gf_ref, gb_ref, *, dk, dv, qscale):
    hn = _norm_mod(x_ref[0], g_ref[...], sc_ref[0], sh_ref[0]).astype(BF16)
    q_ref[0] = _bdot(hn, w_ref[:, 0:dk]) * qscale
    k_ref[0] = _bdot(hn, w_ref[:, dk:2 * dk])
    v_ref[0] = _bdot(hn, w_ref[:, 2 * dk:2 * dk + dv])
    r_ref[0] = _bdot(hn, w_ref[:, 2 * dk + dv:2 * dk + 2 * dv])
    low = _bdot(hn, wl_ref[...])
    gates = _log_sigmoid(_dot3(low, wg_ref[0], wg_ref[1]) + bg_ref[...]) * (1.0 / GLA_TAU)
    gf_ref[0] = gates[:, 0:dk]
    gb_ref[0] = gates[:, dk:2 * dk]


def _gla_in(x, g, sh, sc, w_main_bf, w_low_bf, w_gate, b_gate, dk, dv, tm=512):
    b, s, d = x.shape
    tm = min(tm, s)
    qscale = float((dk // GLA_HEADS) ** -0.5)
    kern = functools.partial(_gla_in_kernel, dk=dk, dv=dv, qscale=qscale)
    tokk = pl.BlockSpec((1, tm, dk), lambda i, j: (i, j, 0))
    tokv = pl.BlockSpec((1, tm, dv), lambda i, j: (i, j, 0))
    return pl.pallas_call(
        kern,
        out_shape=(jax.ShapeDtypeStruct((b, s, dk), F32),
                   jax.ShapeDtypeStruct((b, s, dk), F32),
                   jax.ShapeDtypeStruct((b, s, dv), F32),
                   jax.ShapeDtypeStruct((b, s, dv), F32),
                   jax.ShapeDtypeStruct((b, s, dk), F32),
                   jax.ShapeDtypeStruct((b, s, dk), F32)),
        grid=(b, s // tm),
        in_specs=[
            pl.BlockSpec((1, tm, d), lambda i, j: (i, j, 0)),
            pl.BlockSpec((1, d), lambda i, j: (0, 0)),
            pl.BlockSpec((1, 1, d), lambda i, j: (i, 0, 0)),
            pl.BlockSpec((1, 1, d), lambda i, j: (i, 0, 0)),
            pl.BlockSpec((d, 2 * dk + 2 * dv), lambda i, j: (0, 0)),
            pl.BlockSpec((d, LANES), lambda i, j: (0, 0)),
            pl.BlockSpec((2, LANES, 2 * dk), lambda i, j: (0, 0, 0)),
            pl.BlockSpec((1, 2 * dk), lambda i, j: (0, 0)),
        ],
        out_specs=(tokk, tokk, tokv, tokv, tokk, tokk),
        compiler_params=_params(("parallel", "parallel"), 56),
        name="gla_in",
    )(x, g, sh, sc, w_main_bf, w_low_bf, w_gate, b_gate)


def _gla_core_kernel(q_ref, k_ref, v_ref, gf_ref, gb_ref, o_ref, sf_ref, sb_ref,
                     *, ngroup):
    C = GLA_CHUNK
    G = GLA_GROUP
    R = C * G
    ri = lax.broadcasted_iota(I32, (C, C), 0)
    ci = lax.broadcasted_iota(I32, (C, C), 1)
    mask_f = ci <= ri
    mask_b = ci > ri
    rowc = lax.broadcasted_iota(I32, (R, 1), 0) % C
    sf_ref[...] = jnp.zeros_like(sf_ref)
    sb_ref[...] = jnp.zeros_like(sb_ref)
    nt = (((1,), (1,)), ((), ()))
    tn = (((0,), (0,)), ((), ()))

    def chunk_cumsum(x, reverse):
        sh = 1
        while sh < C:
            if reverse:
                x = x + jnp.where(rowc < C - sh, pltpu.roll(x, R - sh, 0), 0.0)
            else:
                x = x + jnp.where(rowc >= sh, pltpu.roll(x, sh, 0), 0.0)
            sh *= 2
        return x

    def group(gi, g_ref, s_ref, reverse, accumulate):
        rows = pl.ds(pl.multiple_of(gi * R, R), R)
        q = q_ref[0, rows, :]
        k = k_ref[0, rows, :]
        v = v_ref[0, rows, :].astype(BF16)
        cum = chunk_cumsum(g_ref[0, rows, :], reverse)
        q_dec = (q * jnp.exp(cum)).astype(BF16)
        k_inv = (k * jnp.exp(-cum)).astype(BF16)
        st = s_ref[...]
        outs = [None] * G
        for n in (range(G - 1, -1, -1) if reverse else range(G)):
            sl = slice(n * C, (n + 1) * C)
            edge = n * C if reverse else n * C + C - 1
            tot = cum[edge:edge + 1, :]
            k_end = (k[sl] * jnp.exp(tot - cum[sl])).astype(BF16)
            att = lax.dot_general(q_dec[sl], k_inv[sl], nt, preferred_element_type=F32)
            att = jnp.where(mask_b if reverse else mask_f, att, 0.0).astype(BF16)
            outs[n] = _bdot(att, v[sl]) + lax.dot_general(
                q_dec[sl], st.astype(BF16), nt, preferred_element_type=F32)
            st = st * jnp.exp(tot) + lax.dot_general(
                v[sl], k_end, tn, preferred_element_type=F32)
        s_ref[...] = st
        o = jnp.concatenate(outs, axis=0)
        if accumulate:
            o_ref[0, rows, :] += o
        else:
            o_ref[0, rows, :] = o

    def first_half(i, carry):
        group(i, gf_ref, sf_ref, False, False)
        group(ngroup - 1 - i, gb_ref, sb_ref, True, False)
        return carry

    def second_half(i, carry):
        group(i, gf_ref, sf_ref, False, True)
        group(ngroup - 1 - i, gb_ref, sb_ref, True, True)
        return carry

    half = ngroup // 2
    lax.fori_loop(0, half, first_half, 0)
    lax.fori_loop(half, ngroup, second_half, 0)


def _gla_core(q, k, v, gf, gb):
    b, s, dk = q.shape
    dv = v.shape[-1]
    hk = dk // GLA_HEADS
    hv = dv // GLA_HEADS
    ngroup = s // (GLA_CHUNK * GLA_GROUP)
    assert ngroup % 2 == 0
    kern = functools.partial(_gla_core_kernel, ngroup=ngroup)
    speck = pl.BlockSpec((1, s, hk), lambda i, h: (i, 0, h))
    specv = pl.BlockSpec((1, s, hv), lambda i, h: (i, 0, h))
    return pl.pallas_call(
        kern,
        out_shape=jax.ShapeDtypeStruct((b, s, dv), F32),
        grid=(b, GLA_HEADS),
        in_specs=[speck, speck, specv, speck, speck],
        out_specs=specv,
        scratch_shapes=[pltpu.VMEM((hv, hk), F32), pltpu.VMEM((hv, hk), F32)],
        compiler_params=_params(("parallel", "parallel"), 48),
        name="gla_core",
    )(q, k, v, gf, gb)


def _gla_out_kernel(o_ref, r_ref, ng_ref, w_ref, x_ref, gt_ref, g2_ref, sh2_ref,
                    sc2_ref, wr_ref, xo_ref, hn_ref, lg_ref, *, hv):
    o = o_ref[0]
    parts = []
    for h in range(GLA_HEADS):
        oh = o[:, h * hv:(h + 1) * hv]
        ms = jnp.mean(oh * oh, axis=-1, keepdims=True)
        parts.append(oh * lax.rsqrt(ms + EPS))
    on = jnp.concatenate(parts, axis=-1) * ng_ref[...]
    a = on * _silu(r_ref[0])
    _mixer_tail(a.astype(BF16), w_ref, x_ref, gt_ref, g2_ref, sh2_ref, sc2_ref,
                wr_ref, xo_ref, hn_ref, lg_ref)


def _gla_out(o, r, ng, w_bf, x, gt1, g2, sh2, sc2, wr, tm=512):
    b, s, d = x.shape
    dv = o.shape[-1]
    tm = min(tm, s)
    tok, vec, bvec, wsq, wrs, out_shape, out_specs = _tail_specs(b, s, d, tm)
    tokv = pl.BlockSpec((1, tm, dv), lambda i, j: (i, j, 0))
    kern = functools.partial(_gla_out_kernel, hv=dv // GLA_HEADS)
    return pl.pallas_call(
        kern,
        out_shape=out_shape,
        grid=(b, s // tm),
        in_specs=[tokv, tokv, pl.BlockSpec((1, dv), lambda i, j: (0, 0)),
                  pl.BlockSpec((dv, d), lambda i, j: (0, 0)),
                  tok, bvec, vec, bvec, bvec, wrs],
        out_specs=out_specs,
        compiler_params=_params(("parallel", "parallel"), 48),
        name="gla_out",
    )(o, r, ng, w_bf, x, gt1, g2, sh2, sc2, wr)


def _prefix_sum_lanes(x, n):
    lane = lax.broadcasted_iota(I32, x.shape, 1)
    sh = 1
    while sh < n:
        x = x + jnp.where(lane >= sh, pltpu.roll(x, sh, 1), 0)
        sh *= 2
    return x


def _route_kernel(lg_ref, slot_t_ref, slot_ref, aff_ref, toff_ref, *, cap, seq):
    E = N_EXPERTS
    lg = lg_ref[0]
    lane = lax.broadcasted_iota(I32, lg.shape, 1)
    live = lane < E
    lgm = jnp.where(live, lg, -jnp.inf)
    mx = jnp.max(lgm, axis=-1, keepdims=True)
    ex = jnp.where(live, jnp.exp(lgm - mx), 0.0)
    aff = ex / jnp.sum(ex, axis=-1, keepdims=True)
    aff_t = aff.T[0:E, :]
    aff_ref[0] = aff_t
    bits = pltpu.bitcast(aff_t, I32)

    def search(_, lohi):
        lo, hi = lohi
        mid = lo + (hi - lo + 1) // 2
        cnt = jnp.sum((bits >= mid).astype(I32), axis=1, keepdims=True)
        ok = cnt >= cap
        return jnp.where(ok, mid, lo), jnp.where(ok, hi, mid - 1)

    lo0 = jnp.zeros((E, 1), I32)
    hi0 = jnp.full((E, 1), 0x7F800000, I32)
    thr, _ = lax.fori_loop(0, 32, search, (lo0, hi0))
    gt = bits > thr
    eq = bits == thr
    need = cap - jnp.sum(gt.astype(I32), axis=1, keepdims=True)
    eq_i = eq.astype(I32)
    eq_rank = _prefix_sum_lanes(eq_i, seq) - eq_i
    sel = jnp.logical_or(gt, jnp.logical_and(eq, eq_rank < need))
    sel_i = sel.astype(I32)
    pos = _prefix_sum_lanes(sel_i, seq) - sel_i
    slot_t = jnp.where(sel, pos, -1)
    slot_t_ref[0] = slot_t
    pad = jnp.full((LANES - E, seq), -1, I32)
    slot_ref[0] = jnp.concatenate([slot_t, pad], axis=0).T
    lane_e = lax.broadcasted_iota(I32, (E, LANES), 1)
    run = jnp.zeros((E, 1), I32)
    toff = jnp.zeros((E, LANES), I32)
    for j in range(seq // MOE_TILE):
        run = run + jnp.sum(sel_i[:, j * MOE_TILE:(j + 1) * MOE_TILE], axis=1, keepdims=True)
        toff = toff + jnp.where(lane_e == j + 1, run, 0)
    toff_ref[0] = toff


def _route(logits, cap):
    b, s, _ = logits.shape
    assert s // MOE_TILE < LANES
    kern = functools.partial(_route_kernel, cap=cap, seq=s)
    return pl.pallas_call(
        kern,
        out_shape=(jax.ShapeDtypeStruct((b, N_EXPERTS, s), I32),
                   jax.ShapeDtypeStruct((b, s, LANES), I32),
                   jax.ShapeDtypeStruct((b, N_EXPERTS, s), F32),
                   jax.ShapeDtypeStruct((b, N_EXPERTS, LANES), I32)),
        grid=(b,),
        in_specs=[pl.BlockSpec((1, s, LANES), lambda i: (i, 0, 0))],
        out_specs=(pl.BlockSpec((1, N_EXPERTS, s), lambda i: (i, 0, 0)),
                   pl.BlockSpec((1, s, LANES), lambda i: (i, 0, 0)),
                   pl.BlockSpec((1, N_EXPERTS, s), lambda i: (i, 0, 0)),
                   pl.BlockSpec((1, N_EXPERTS, LANES), lambda i: (i, 0, 0))),
        compiler_params=_params(("parallel",), 48),
        name="moe_route",
    )(logits)


def _tile_range(toff_ref, base, nt, lo, hi):
    jlo = jnp.int32(0)
    jend = jnp.int32(0)
    for j in range(nt):
        jlo = jlo + jnp.where(toff_ref[base + j + 1] <= lo, 1, 0)
        jend = jend + jnp.where(toff_ref[base + j] < hi, 1, 0)
    return jlo, jend


def _moe_ffn_kernel(toff_ref, hn_ref, slot_ref, aff_ref, wg_ref, wu_ref, wd_ref,
                    y_ref, xg_ref, acc_ref, gacc_ref, gate_ref, *, cap, seq):
    f = pl.program_id(2)
    nf = pl.num_programs(2)

    @pl.when(f == 0)
    def _():
        base = (pl.program_id(0) * pl.num_programs(1) + pl.program_id(1)) * LANES
        nt = seq // MOE_TILE
        wt = min(MOE_WIN_TILES, nt)
        r = lax.broadcasted_iota(I32, (MOE_SLOT_BLOCK, 1), 0)

        def gathered(t0, width, r0):
            hit = slot_ref[0, 0, :, pl.ds(t0, width)] == r + r0
            onehot = jnp.where(hit, 1.0, 0.0).astype(BF16)
            gate = jnp.sum(jnp.where(hit, aff_ref[0, 0, :, pl.ds(t0, width)], 0.0),
                           axis=1, keepdims=True)
            return _bdot(onehot, hn_ref[0, pl.ds(t0, width), :]), gate

        for kb in range(cap // MOE_SLOT_BLOCK):
            r0 = kb * MOE_SLOT_BLOCK
            rows = slice(r0, r0 + MOE_SLOT_BLOCK)
            jlo, jend = _tile_range(toff_ref, base, nt, r0, r0 + MOE_SLOT_BLOCK)
            jw = jnp.minimum(jlo, nt - wt)
            xw, gw = gathered(pl.multiple_of(jw * MOE_TILE, MOE_TILE), wt * MOE_TILE, r0)
            gacc_ref[...] = xw
            gate_ref[rows, :] = gw

            def tile(j, carry, rows=rows, r0=r0):
                xj, gj = gathered(pl.multiple_of(j * MOE_TILE, MOE_TILE), MOE_TILE, r0)
                gacc_ref[...] += xj
                gate_ref[rows, :] += gj
                return carry

            lax.fori_loop(jnp.maximum(jlo, jw + wt), jend, tile, 0)
            xg_ref[rows, :] = gacc_ref[...].astype(BF16)
        acc_ref[...] = jnp.zeros_like(acc_ref)

    xg = xg_ref[...]
    a = _bdot(xg, wg_ref[0])
    u = _bdot(xg, wu_ref[0])
    hh = (_silu(a) * u).astype(BF16)
    acc_ref[...] += _bdot(hh, wd_ref[0])

    @pl.when(f == nf - 1)
    def _():
        y_ref[0, 0] = (acc_ref[...] * gate_ref[...]).astype(BF16)


def _moe_ffn(hn_bf, slot_t, aff_t, toff, wg_bf, wu_bf, wd_bf, cap, fc=1024):
    b, s, d = hn_bf.shape
    e, _, fdim = wg_bf.shape
    fc = min(fc, fdim)
    assert cap % MOE_SLOT_BLOCK == 0 and s % MOE_TILE == 0
    kern = functools.partial(_moe_ffn_kernel, cap=cap, seq=s)
    grid_spec = pltpu.PrefetchScalarGridSpec(
        num_scalar_prefetch=1,
        grid=(b, e, fdim // fc),
        in_specs=[
            pl.BlockSpec((1, s, d), lambda i, j, f, t: (i, 0, 0)),
            pl.BlockSpec((1, 1, 1, s), lambda i, j, f, t: (i, j, 0, 0)),
            pl.BlockSpec((1, 1, 1, s), lambda i, j, f, t: (i, j, 0, 0)),
            pl.BlockSpec((1, d, fc), lambda i, j, f, t: (j, 0, f)),
            pl.BlockSpec((1, d, fc), lambda i, j, f, t: (j, 0, f)),
            pl.BlockSpec((1, fc, d), lambda i, j, f, t: (j, f, 0)),
        ],
        out_specs=pl.BlockSpec((1, 1, cap, d), lambda i, j, f, t: (i, j, 0, 0)),
        scratch_shapes=[pltpu.VMEM((cap, d), BF16), pltpu.VMEM((cap, d), F32),
                        pltpu.VMEM((MOE_SLOT_BLOCK, d), F32), pltpu.VMEM((cap, 1), F32)],
    )
    return pl.pallas_call(
        kern,
        out_shape=jax.ShapeDtypeStruct((b, e, cap, d), BF16),
        grid_spec=grid_spec,
        compiler_params=_params(("parallel", "arbitrary", "arbitrary"), 56),
        name="moe_ffn",
    )(toff.reshape(-1), hn_bf, slot_t.reshape(b, e, 1, s), aff_t.reshape(b, e, 1, s),
      wg_bf, wu_bf, wd_bf)


def _moe_combine_kernel(toff_ref, x_ref, gt_ref, slot_ref, y_ref, gfin_ref, o_ref,
                        ycat_ref, oh_ref, acc_ref, *, cap, final, tt, ne):
    nsub = tt // MOE_TILE
    kw = min(MOE_TILE, cap)
    slot = slot_ref[0]
    unfit = []
    for e in range(ne):
        base = (pl.program_id(0) * ne + e) * LANES + pl.program_id(1) * nsub
        lo = toff_ref[base]
        hi = toff_ref[base + nsub]
        start = jnp.minimum((lo // BF16_ROWS) * BF16_ROWS, cap - kw)
        s0 = pl.multiple_of(start, BF16_ROWS)
        fits = hi <= start + kw
        unfit.append(jnp.logical_not(fits))
        ycat_ref[e * kw:(e + 1) * kw, :] = y_ref[0, e, pl.ds(s0, kw), :]
        r = jnp.where(fits, lax.broadcasted_iota(I32, (1, kw), 1) + s0, -2)
        hit = slot[:, e:e + 1] == r
        oh_ref[:, e * kw:(e + 1) * kw] = jnp.where(hit, 1.0, 0.0).astype(BF16)
    acc_ref[...] = _bdot(oh_ref[...], ycat_ref[...])

    for e in range(ne):

        @pl.when(unfit[e])
        def _(e=e):
            r = lax.broadcasted_iota(I32, (1, cap), 1)
            onehot = jnp.where(slot[:, e:e + 1] == r, 1.0, 0.0).astype(BF16)
            acc_ref[...] += _bdot(onehot, y_ref[0, e])

    xn = x_ref[0] + gt_ref[0] * acc_ref[...]
    if final:
        ms = jnp.mean(xn * xn, axis=-1, keepdims=True)
        xn = xn * lax.rsqrt(ms + EPS) * gfin_ref[...]
    o_ref[0] = xn


def _moe_combine(x, gt2, slot, toff, y, gfin, cap, final, tt=512):
    b, s, d = x.shape
    e = y.shape[1]
    tt = min(tt, s)
    kw = min(MOE_TILE, cap)
    assert tt % MOE_TILE == 0 and cap % kw == 0
    kern = functools.partial(_moe_combine_kernel, cap=cap, final=final, tt=tt, ne=e)
    grid_spec = pltpu.PrefetchScalarGridSpec(
        num_scalar_prefetch=1,
        grid=(b, s // tt),
        in_specs=[
            pl.BlockSpec((1, tt, d), lambda i, t, o: (i, t, 0)),
            pl.BlockSpec((1, 1, d), lambda i, t, o: (i, 0, 0)),
            pl.BlockSpec((1, tt, LANES), lambda i, t, o: (i, t, 0)),
            pl.BlockSpec((1, e, cap, d), lambda i, t, o: (i, 0, 0, 0),
                         pipeline_mode=pl.Buffered(1)),
            pl.BlockSpec((1, d), lambda i, t, o: (0, 0)),
        ],
        out_specs=pl.BlockSpec((1, tt, d), lambda i, t, o: (i, t, 0)),
        scratch_shapes=[pltpu.VMEM((e * kw, d), BF16), pltpu.VMEM((tt, e * kw), BF16),
                        pltpu.VMEM((tt, d), F32)],
    )
    return pl.pallas_call(
        kern,
        out_shape=jax.ShapeDtypeStruct((b, s, d), F32),
        grid_spec=grid_spec,
        compiler_params=_params(("parallel", "arbitrary"), 56),
        name="moe_combine",
    )(toff.reshape(-1), x, gt2, slot, y, gfin)


def _hi_lo(w):
    hi = w.astype(BF16)
    return jnp.stack([hi, (w - hi.astype(F32)).astype(BF16)])


def kernel(x, c, w_ada, b_ada, norm_mix, norm_ffn, hy_w_in, hy_conv_w, hy_conv_b, hy_f_w1, hy_f_b1, hy_f_w2, hy_f_b2, hy_f_w3, hy_f_freq, hy_skip, hy_w_out, gla_w_in, gla_w_gf, gla_b_gf, gla_w_gb, gla_b_gb, gla_norm, gla_w_out, moe_router, moe_w_gate, moe_w_up, moe_w_down, norm_final):
    b, s, d = x.shape
    depth = w_ada.shape[0]
    n_exp = moe_router.shape[-1]
    cap = EC_CAPACITY * s // n_exp
    dk = gla_w_gf.shape[-1]
    dv = gla_w_out.shape[1]
    nb = s // CONV_BLOCK

    mod = _adaln(c, w_ada, b_ada)
    for i in range(depth):
        j = i // 2
        sh1, sc1, gt1, sh2, sc2, gt2 = (mod[i, m].reshape(b, 1, d) for m in range(N_MOD))
        g1 = norm_mix[i][None, :]
        g2 = norm_ffn[i][None, :]
        wr = _hi_lo(jnp.zeros((d, LANES), F32).at[:, :n_exp].set(moe_router[i]))
        if i % 2 == 0:
            x0, u = _hy_in(x, g1, sh1, sc1, hy_w_in[j].astype(BF16), hy_conv_w[j],
                           hy_conv_b[j][None, :])
            gt, nrm = _hy_filter(s, hy_f_w1[j], hy_f_b1[j], hy_f_w2[j], hy_f_b2[j],
                                 hy_f_w3[j], hy_f_freq[j])
            ut = u.reshape(b, nb, CONV_BLOCK, d).transpose(3, 1, 0, 2)
            yt = _hy_conv(gt, nrm, ut.reshape(d, nb * b, CONV_BLOCK), nb, b)
            y = yt.reshape(d, nb, b, CONV_BLOCK).transpose(2, 1, 3, 0).reshape(b, s, d)
            x, hn, lg = _hy_out(y, u, x0, hy_skip[j][None, :], hy_w_out[j].astype(BF16),
                                x, gt1, g2, sh2, sc2, wr)
        else:
            w_in = gla_w_in[j]
            w_low = jnp.zeros((d, LANES), F32).at[:, :2 * GLA_GATE_RANK].set(
                w_in[:, 2 * dk + 2 * dv:]).astype(BF16)
            w_gate = jnp.zeros((LANES, 2 * dk), F32)
            w_gate = w_gate.at[:GLA_GATE_RANK, :dk].set(gla_w_gf[j])
            w_gate = w_gate.at[GLA_GATE_RANK:2 * GLA_GATE_RANK, dk:].set(gla_w_gb[j])
            b_gate = jnp.concatenate([gla_b_gf[j], gla_b_gb[j]])[None, :]
            q, k, v, r, gf, gb = _gla_in(x, g1, sh1, sc1,
                                         w_in[:, :2 * dk + 2 * dv].astype(BF16),
                                         w_low, _hi_lo(w_gate), b_gate, dk, dv)
            o = _gla_core(q, k, v, gf, gb)
            x, hn, lg = _gla_out(o, r, gla_norm[j].reshape(1, dv),
                                 gla_w_out[j].astype(BF16), x, gt1, g2, sh2, sc2, wr)
        slot_t, slot, aff_t, toff = _route(lg, cap)
        y = _moe_ffn(hn, slot_t, aff_t, toff, moe_w_gate[i].astype(BF16),
                     moe_w_up[i].astype(BF16), moe_w_down[i].astype(BF16), cap)
        x = _moe_combine(x, gt2, slot, toff, y, norm_final[None, :], cap,
                         final=(i == depth - 1))
    return x
```
